```python
import math
import jax
import jax.numpy as jnp
from jax import lax
import numpy as np

D_MODEL = 1024
BATCH = 4
SEQ = 4096
DEPTH = 4
DEC_BATCH = 128
DEC_SEQ = 1
PAST_LEN = 8192
PAGE_SIZE = 128

HEAD_DIM = 64
SB_HEADS = 4
SB_KV_HEADS = 2
MLA_HEADS = 4
MLA_Q_LORA = 256
MLA_KV_LORA = 128
MLA_NOPE = 64
MLA_ROPE = 32
MLA_V = 64
ROPE_BASE = 10000.0
NSA_HEADS = 8
CMP_STRIDE = 16
CMP_LEN = 2 * CMP_STRIDE
CMP_HID = 256
SEL_BLOCK = 64
CH_PER_SEL = SEL_BLOCK // CMP_STRIDE
SEL_TOPK = 16
WINDOW = 512
N_BUCKETS = 32
MAX_DISTANCE = 128
MIX_WIDTH = SB_HEADS * HEAD_DIM + MLA_HEADS * MLA_V + NSA_HEADS * HEAD_DIM
D_FF = -(-8 * D_MODEL // (3 * 256)) * 256
IN_SIZES = (SB_HEADS * HEAD_DIM, SB_KV_HEADS * HEAD_DIM, SB_KV_HEADS * HEAD_DIM,
            MLA_Q_LORA, MLA_KV_LORA, MLA_ROPE,
            NSA_HEADS * HEAD_DIM, HEAD_DIM, HEAD_DIM, HEAD_DIM, HEAD_DIM, HEAD_DIM, HEAD_DIM,
            3 * NSA_HEADS)
N_IN = sum(IN_SIZES)
Q_BLOCK = 128
RMS_EPS = 1e-6
NEG_INF = -1e30
FORCE_SCORE = 1e4

kernel_name = 'hybrid_sb_mla_nsa_decoder_step'


def _split_points():
    return np.cumsum(np.array(IN_SIZES))[:-1].tolist()


def _rmsnorm(x, g):
    xf = x.astype(jnp.float32)
    y = xf * lax.rsqrt(jnp.mean(xf * xf, axis=-1, keepdims=True) + RMS_EPS)
    return (y * g.astype(jnp.float32)).astype(x.dtype)


def _rope(x, pos):
    half = x.shape[-1] // 2
    inv = ROPE_BASE ** (-jnp.arange(half, dtype=jnp.float32) / half)
    ang = pos.astype(jnp.float32)[:, None] * inv[None, :]
    shape = (1, pos.shape[0]) + (1,) * (x.ndim - 3) + (half,)
    cos = jnp.cos(ang).reshape(shape)
    sin = jnp.sin(ang).reshape(shape)
    x1 = x[..., :half].astype(jnp.float32)
    x2 = x[..., half:].astype(jnp.float32)
    return jnp.concatenate([x1 * cos - x2 * sin, x2 * cos + x1 * sin], axis=-1).astype(x.dtype)


def _t5_bucket(dist):
    exact = N_BUCKETS // 2
    d = jnp.maximum(dist, 0)
    ratio = jnp.log(jnp.maximum(d, 1).astype(jnp.float32) / exact) / math.log(MAX_DISTANCE / exact)
    large = jnp.minimum(exact + (ratio * (N_BUCKETS - exact)).astype(jnp.int32), N_BUCKETS - 1)
    return jnp.where(d < exact, d, large)


def _query_blocks(fn, T):
    qb = Q_BLOCK if T % Q_BLOCK == 0 else T
    starts = jnp.arange(T // qb, dtype=jnp.int32) * qb
    out = lax.map(lambda s: fn(s, qb), starts)
    out = jnp.moveaxis(out, 0, 1)
    return out.reshape((out.shape[0], T) + out.shape[3:])


def _stick_breaking(q, k, v, pos_q, pos_k):
    B, T = q.shape[:2]
    G = SB_HEADS // SB_KV_HEADS
    qg = q.reshape(B, T, SB_KV_HEADS, G, HEAD_DIM)
    scale = HEAD_DIM ** -0.5

    def blk(s, qb):
        qs = lax.dynamic_slice_in_dim(qg, s, qb, 1)
        ps = lax.dynamic_slice_in_dim(pos_q, s, qb, 0)
        z = jnp.einsum('bqhgd,bkhd->bhgqk', qs, k, preferred_element_type=jnp.float32) * scale
        mask = pos_k[None, :] < ps[:, None]
        log_rest = jnp.where(mask, jax.nn.log_sigmoid(-z), 0.0)
        after = lax.cumsum(log_rest, axis=4, reverse=True) - log_rest
        w = jnp.where(mask, jnp.exp(jax.nn.log_sigmoid(z) + after), 0.0)
        o = jnp.einsum('bhgqk,bkhd->bqhgd', w.astype(v.dtype), v)
        return o.reshape(B, qb, SB_HEADS * HEAD_DIM)

    return _query_blocks(blk, T)


def _mla(c_q, c_kv, k_r, past_ckv, past_kr, pos_q, pos_k, g_q, w_uq, g_kv, w_uk, w_uv):
    B, T = c_q.shape[:2]
    qf = (_rmsnorm(c_q, g_q) @ w_uq).reshape(B, T, MLA_HEADS, MLA_NOPE + MLA_ROPE)
    q_nope = qf[..., :MLA_NOPE]
    q_rope = _rope(qf[..., MLA_NOPE:], pos_q)
    ckv_new = _rmsnorm(c_kv, g_kv)
    kr_new = _rope(k_r, pos_q)
    ckv_all = jnp.concatenate([past_ckv, ckv_new], axis=1)
    kr_all = jnp.concatenate([past_kr, kr_new], axis=1)
    wuk = w_uk.reshape(MLA_KV_LORA, MLA_HEADS, MLA_NOPE)
    wuv = w_uv.reshape(MLA_KV_LORA, MLA_HEADS, MLA_V)
    q_lat = jnp.einsum('bthn,chn->bthc', q_nope, wuk)
    scale = (MLA_NOPE + MLA_ROPE) ** -0.5

    def blk(s, qb):
        ql = lax.dynamic_slice_in_dim(q_lat, s, qb, 1)
        qr = lax.dynamic_slice_in_dim(q_rope, s, qb, 1)
        ps = lax.dynamic_slice_in_dim(pos_q, s, qb, 0)
        sc = (jnp.einsum('bqhc,bkc->bhqk', ql, ckv_all, preferred_element_type=jnp.float32)
              + jnp.einsum('bqhr,bkr->bhqk', qr, kr_all, preferred_element_type=jnp.float32)) * scale
        mask = pos_k[None, :] <= ps[:, None]
        p = jax.nn.softmax(jnp.where(mask, sc, NEG_INF), axis=-1)
        o_lat = jnp.einsum('bhqk,bkc->bqhc', p.astype(ckv_all.dtype), ckv_all)
        return jnp.einsum('bqhc,chv->bqhv', o_lat, wuv).reshape(B, qb, MLA_HEADS * MLA_V)

    return _query_blocks(blk, T), ckv_new, kr_new


def _compress(x, pos_emb, w1, w2):
    B, L, _ = x.shape
    nch = L // CMP_STRIDE
    ch = x[:, :nch * CMP_STRIDE].reshape(B, nch, CMP_STRIDE, HEAD_DIM)
    first = jnp.einsum('bnsd,sdh->bnh', ch + pos_emb[:CMP_STRIDE], w1[:CMP_STRIDE])
    second = jnp.einsum('bnsd,sdh->bnh', ch + pos_emb[CMP_STRIDE:], w1[CMP_STRIDE:])
    hid = jax.nn.silu(first[:, :-1] + second[:, 1:])
    return hid @ w2


def _nsa(q, ck, cv, sk, sv, wk_ctx, wv_ctx, gate_logits, pos_q, pos0, rel_bias,
         pos_emb_k, w1_k, w2_k, pos_emb_v, w1_v, w2_v):
    B, T = q.shape[:2]
    L = ck.shape[1]
    kc = _compress(ck, pos_emb_k, w1_k, w2_k)
    vc = _compress(cv, pos_emb_v, w1_v, w2_v)
    n_cmp = kc.shape[1]
    cmp_end = (jnp.arange(n_cmp, dtype=jnp.int32) + 2) * CMP_STRIDE - 1
    n_sel = -(-L // SEL_BLOCK)
    pad = n_sel * SEL_BLOCK - L
    kb = jnp.pad(sk, ((0, 0), (0, pad), (0, 0))).reshape(B, n_sel, SEL_BLOCK, HEAD_DIM)
    vb = jnp.pad(sv, ((0, 0), (0, pad), (0, 0))).reshape(B, n_sel, SEL_BLOCK, HEAD_DIM)
    k_top = min(SEL_TOPK, n_sel)
    sel_idx = jnp.arange(n_sel, dtype=jnp.int32)
    p_w = wk_ctx.shape[1] - T
    win_pos0 = pos0 - p_w
    wkp = jnp.pad(wk_ctx, ((0, 0), (WINDOW, 0), (0, 0)))
    wvp = jnp.pad(wv_ctx, ((0, 0), (WINDOW, 0), (0, 0)))
    gates = jax.nn.sigmoid(gate_logits.astype(jnp.float32)).reshape(B, T, 3, NSA_HEADS)
    scale = HEAD_DIM ** -0.5
    bidx = jnp.arange(B)[:, None, None]

    def bias(dist):
        return jnp.moveaxis(rel_bias[_t5_bucket(dist)], -1, -3).astype(jnp.float32)

    def blk(s, qb):
        qs = lax.dynamic_slice_in_dim(q, s, qb, 1)
        ps = lax.dynamic_slice_in_dim(pos_q, s, qb, 0)
        gs = lax.dynamic_slice_in_dim(gates, s, qb, 1)
        okc = cmp_end[None, :] <= ps[:, None]
        sc = jnp.einsum('bqhd,bnd->bhqn', qs, kc, preferred_element_type=jnp.float32) * scale
        sc = sc + bias(ps[:, None] - cmp_end[None, :])
        pc = jnp.where(okc, jax.nn.softmax(jnp.where(okc, sc, NEG_INF), axis=-1), 0.0)
        oc = jnp.einsum('bhqn,bnd->bqhd', pc.astype(vc.dtype), vc)
        imp = pc.sum(axis=1)
        chunk = 0.5 * (jnp.pad(imp, ((0, 0), (0, 0), (0, 1))) + jnp.pad(imp, ((0, 0), (0, 0), (1, 0))))
        chunk = jnp.pad(chunk, ((0, 0), (0, 0), (0, n_sel * CH_PER_SEL - n_cmp - 1)))
        p_slc = chunk.reshape(B, qb, n_sel, CH_PER_SEL).sum(-1)
        cur = ps // SEL_BLOCK
        forced = ((sel_idx[None, :] == 0) | (sel_idx[None, :] == cur[:, None])
                  | (sel_idx[None, :] == cur[:, None] - 1))
        causal = sel_idx[None, :] * SEL_BLOCK <= ps[:, None]
        score = jnp.where(causal, jnp.where(forced, FORCE_SCORE, p_slc), -1.0)
        top_v, top_i = lax.top_k(score, k_top)
        kg = kb[bidx, top_i].reshape(B, qb, k_top * SEL_BLOCK, HEAD_DIM)
        vg = vb[bidx, top_i].reshape(B, qb, k_top * SEL_BLOCK, HEAD_DIM)
        kpos = top_i[..., None] * SEL_BLOCK + jnp.arange(SEL_BLOCK, dtype=jnp.int32)
        oks = ((top_v >= 0.0)[..., None] & (kpos <= ps[None, :, None, None])).reshape(B, qb, -1)
        kpos = kpos.reshape(B, qb, -1)
        ss = jnp.einsum('bqhd,bqkd->bhqk', qs, kg, preferred_element_type=jnp.float32) * scale
        ss = ss + bias(ps[None, :, None] - kpos)
        psel = jax.nn.softmax(jnp.where(oks[:, None], ss, NEG_INF), axis=-1)
        osel = jnp.einsum('bhqk,bqkd->bqhd', psel.astype(vg.dtype), vg)
        kw = lax.dynamic_slice_in_dim(wkp, p_w + s, WINDOW + qb, 1)
        vw = lax.dynamic_slice_in_dim(wvp, p_w + s, WINDOW + qb, 1)
        cidx = p_w + s - WINDOW + jnp.arange(WINDOW + qb, dtype=jnp.int32)
        kposw = win_pos0 + cidx
        okw = ((cidx[None, :] >= 0) & (kposw[None, :] <= ps[:, None])
               & (kposw[None, :] > ps[:, None] - WINDOW))
        sw = jnp.einsum('bqhd,bkd->bhqk', qs, kw, preferred_element_type=jnp.float32) * scale
        sw = sw + bias(ps[:, None] - kposw[None, :])
        pw = jax.nn.softmax(jnp.where(okw, sw, NEG_INF), axis=-1)
        ow = jnp.einsum('bhqk,bkd->bqhd', pw.astype(vw.dtype), vw)
        o = gs[:, :, 0, :, None] * oc + gs[:, :, 1, :, None] * osel + gs[:, :, 2, :, None] * ow
        return o.reshape(B, qb, NSA_HEADS * HEAD_DIM).astype(q.dtype)

    return _query_blocks(blk, T)


def _layer(x, past, pos0, W, l):
    (p_sbk, p_sbv, p_ckv, p_kr, p_ck, p_cv, p_sk, p_sv, p_wk, p_wv) = past
    B, T, _ = x.shape
    pos_q = pos0 + jnp.arange(T, dtype=jnp.int32)
    pos_k = jnp.arange(pos0 + T, dtype=jnp.int32)
    h = _rmsnorm(x, W['g_attn_pre'][l])
    proj = h @ W['w_in'][l]
    (sb_q, sb_k, sb_v, c_q, c_kv, k_r, n_q, ck, cv, sk, sv, wk, wv, g_logit) = jnp.split(
        proj, _split_points(), axis=-1)
    sb_q = sb_q.reshape(B, T, SB_HEADS, HEAD_DIM)
    sb_k = sb_k.reshape(B, T, SB_KV_HEADS, HEAD_DIM)
    sb_v = sb_v.reshape(B, T, SB_KV_HEADS, HEAD_DIM)
    o_a = _stick_breaking(sb_q, jnp.concatenate([p_sbk, sb_k], 1), jnp.concatenate([p_sbv, sb_v], 1),
                          pos_q, pos_k)
    o_b, ckv_new, kr_new = _mla(c_q, c_kv, k_r, p_ckv, p_kr, pos_q, pos_k, W['g_q_lora'][l], W['w_uq'][l],
                                W['g_kv_lora'][l], W['w_uk'][l], W['w_uv'][l])
    wk_ctx = jnp.concatenate([p_wk, wk], 1)
    wv_ctx = jnp.concatenate([p_wv, wv], 1)
    o_c = _nsa(n_q.reshape(B, T, NSA_HEADS, HEAD_DIM),
               jnp.concatenate([p_ck, ck], 1), jnp.concatenate([p_cv, cv], 1),
               jnp.concatenate([p_sk, sk], 1), jnp.concatenate([p_sv, sv], 1),
               wk_ctx, wv_ctx, g_logit, pos_q, pos0, W['rel_bias'],
               W['cmp_pos_k'][l], W['cmp_w1_k'][l], W['cmp_w2_k'][l],
               W['cmp_pos_v'][l], W['cmp_w1_v'][l], W['cmp_w2_v'][l])
    mix = jnp.concatenate([o_a, o_b, o_c], axis=-1)
    x = x + _rmsnorm(mix @ W['w_out'][l], W['g_attn_post'][l])
    hf = _rmsnorm(x, W['g_ffn_pre'][l])
    f = (jax.nn.silu(hf @ W['w_gate'][l]) * (hf @ W['w_up'][l])) @ W['w_down'][l]
    x = x + _rmsnorm(f, W['g_ffn_post'][l])
    keep = min(WINDOW, wk_ctx.shape[1])
    return x, (sb_k, sb_v, ckv_new, kr_new, ck, cv, sk, sv, wk_ctx[:, -keep:], wv_ctx[:, -keep:])


def _empty_past(B, dt):
    def z(*s):
        return jnp.zeros((B, 0) + s, dt)
    return (z(SB_KV_HEADS, HEAD_DIM), z(SB_KV_HEADS, HEAD_DIM), z(MLA_KV_LORA), z(MLA_ROPE),
            z(HEAD_DIM), z(HEAD_DIM), z(HEAD_DIM), z(HEAD_DIM), z(HEAD_DIM), z(HEAD_DIM))


def _paged_past(cache, l, page_table):
    g = cache[l, page_table]
    return g.reshape((g.shape[0], g.shape[1] * g.shape[2]) + g.shape[3:])


def _stack_rows(rows):
    return [jnp.stack([r[i] for r in rows], axis=0) for i in range(len(rows[0]))]


def setup_inputs(seed: int = 0) -> dict:
    key = jax.random.key(seed)
    keys = iter(jax.random.split(key, 48))
    f32 = jnp.float32

    def nrm(shape, scale=1.0):
        return jax.random.normal(next(keys), shape, f32) * scale

    def gain(shape):
        return 1.0 + 0.05 * jax.random.normal(next(keys), shape, f32)

    n_pages = PAST_LEN // PAGE_SIZE
    n_used = DEC_BATCH * n_pages
    n_pool = n_used + (n_used + 3) // 4
    win_buf = min(WINDOW, PAST_LEN)
    perm = jax.random.permutation(next(keys), n_pool)
    page_table = perm[:n_used].reshape(DEC_BATCH, n_pages).astype(jnp.int32)
    pool = (DEPTH, n_pool, PAGE_SIZE)
    return {
        'x_prompt': nrm((BATCH, SEQ, D_MODEL)),
        'x_sample': nrm((DEC_BATCH, DEC_SEQ, D_MODEL)),
        'cache_sb_k': nrm(pool + (SB_KV_HEADS, HEAD_DIM)),
        'cache_sb_v': nrm(pool + (SB_KV_HEADS, HEAD_DIM)),
        'cache_mla_ckv': nrm(pool + (MLA_KV_LORA,)),
        'cache_mla_kr': nrm(pool + (MLA_ROPE,)),
        'cache_nsa_cmp_k': nrm(pool + (HEAD_DIM,)),
        'cache_nsa_cmp_v': nrm(pool + (HEAD_DIM,)),
        'cache_nsa_sel_k': nrm(pool + (HEAD_DIM,)),
        'cache_nsa_sel_v': nrm(pool + (HEAD_DIM,)),
        'state_nsa_win_k': nrm((DEPTH, DEC_BATCH, win_buf, HEAD_DIM)),
        'state_nsa_win_v': nrm((DEPTH, DEC_BATCH, win_buf, HEAD_DIM)),
        'page_table': page_table,
        'w_in': nrm((DEPTH, D_MODEL, N_IN), D_MODEL ** -0.5),
        'g_q_lora': gain((DEPTH, MLA_Q_LORA)),
        'w_uq': nrm((DEPTH, MLA_Q_LORA, MLA_HEADS * (MLA_NOPE + MLA_ROPE)), MLA_Q_LORA ** -0.5),
        'g_kv_lora': gain((DEPTH, MLA_KV_LORA)),
        'w_uk': nrm((DEPTH, MLA_KV_LORA, MLA_HEADS * MLA_NOPE), MLA_KV_LORA ** -0.5),
        'w_uv': nrm((DEPTH, MLA_KV_LORA, MLA_HEADS * MLA_V), MLA_KV_LORA ** -0.5),
        'cmp_pos_k': nrm((DEPTH, CMP_LEN, HEAD_DIM), 0.5),
        'cmp_w1_k': nrm((DEPTH, CMP_LEN, HEAD_DIM, CMP_HID), (CMP_LEN * HEAD_DIM) ** -0.5),
        'cmp_w2_k': nrm((DEPTH, CMP_HID, HEAD_DIM), CMP_HID ** -0.5),
        'cmp_pos_v': nrm((DEPTH, CMP_LEN, HEAD_DIM), 0.5),
        'cmp_w1_v': nrm((DEPTH, CMP_LEN, HEAD_DIM, CMP_HID), (CMP_LEN * HEAD_DIM) ** -0.5),
        'cmp_w2_v': nrm((DEPTH, CMP_HID, HEAD_DIM), CMP_HID ** -0.5),
        'rel_bias': nrm((N_BUCKETS, NSA_HEADS), 0.5),
        'w_out': nrm((DEPTH, MIX_WIDTH, D_MODEL), MIX_WIDTH ** -0.5),
        'g_attn_pre': gain((DEPTH, D_MODEL)),
        'g_attn_post': gain((DEPTH, D_MODEL)),
        'g_ffn_pre': gain((DEPTH, D_MODEL)),
        'g_ffn_post': gain((DEPTH, D_MODEL)),
        'w_gate': nrm((DEPTH, D_MODEL, D_FF), D_MODEL ** -0.5),
        'w_up': nrm((DEPTH, D_MODEL, D_FF), D_MODEL ** -0.5),
        'w_down': nrm((DEPTH, D_FF, D_MODEL), D_FF ** -0.5),
    }


def reference(x_prompt, x_sample, cache_sb_k, cache_sb_v, cache_mla_ckv, cache_mla_kr,
              cache_nsa_cmp_k, cache_nsa_cmp_v, cache_nsa_sel_k, cache_nsa_sel_v,
              state_nsa_win_k, state_nsa_win_v, page_table,
              w_in, g_q_lora, w_uq, g_kv_lora, w_uk, w_uv,
              cmp_pos_k, cmp_w1_k, cmp_w2_k, cmp_pos_v, cmp_w1_v, cmp_w2_v,
              rel_bias, w_out, g_attn_pre, g_attn_post, g_ffn_pre, g_ffn_post,
              w_gate, w_up, w_down):
    W = dict(w_in=w_in, g_q_lora=g_q_lora, w_uq=w_uq, g_kv_lora=g_kv_lora, w_uk=w_uk, w_uv=w_uv,
             cmp_pos_k=cmp_pos_k, cmp_w1_k=cmp_w1_k, cmp_w2_k=cmp_w2_k,
             cmp_pos_v=cmp_pos_v, cmp_w1_v=cmp_w1_v, cmp_w2_v=cmp_w2_v,
             rel_bias=rel_bias, w_out=w_out, g_attn_pre=g_attn_pre, g_attn_post=g_attn_post,
             g_ffn_pre=g_ffn_pre, g_ffn_post=g_ffn_post, w_gate=w_gate, w_up=w_up, w_down=w_down)
    empty = _empty_past(x_prompt.shape[0], x_prompt.dtype)
    y = x_prompt
    rows_p = []
    for l in range(DEPTH):
        y, r = _layer(y, empty, 0, W, l)
        rows_p.append(r)
    y_prompt = y
    past_len = page_table.shape[1] * cache_sb_k.shape[2]
    y = x_sample
    rows_s = []
    for l in range(DEPTH):
        past = (_paged_past(cache_sb_k, l, page_table), _paged_past(cache_sb_v, l, page_table),
                _paged_past(cache_mla_ckv, l, page_table), _paged_past(cache_mla_kr, l, page_table),
                _paged_past(cache_nsa_cmp_k, l, page_table), _paged_past(cache_nsa_cmp_v, l, page_table),
                _paged_past(cache_nsa_sel_k, l, page_table), _paged_past(cache_nsa_sel_v, l, page_table),
                state_nsa_win_k[l], state_nsa_win_v[l])
        y, r = _layer(y, past, past_len, W, l)
        rows_s.append(r)
    y_sample = y
    (sb_k_p, sb_v_p, mla_ckv_p, mla_kr_p, cmp_k_p, cmp_v_p, sel_k_p, sel_v_p,
     win_k_p, win_v_p) = _stack_rows(rows_p)
    (sb_k_s, sb_v_s, mla_ckv_s, mla_kr_s, cmp_k_s, cmp_v_s, sel_k_s, sel_v_s,
     win_k_s, win_v_s) = _stack_rows(rows_s)
    return (y_prompt, y_sample,
            sb_k_p, sb_v_p, mla_ckv_p, mla_kr_p, cmp_k_p, cmp_v_p, sel_k_p, sel_v_p, win_k_p, win_v_p,
            sb_k_s, sb_v_s, mla_ckv_s, mla_kr_s, cmp_k_s, cmp_v_s, sel_k_s, sel_v_s, win_k_s, win_v_s)
```

```python
import functools
import math

import numpy as np
import jax
import jax.numpy as jnp
from jax import lax
from jax.experimental import pallas as pl
from jax.experimental.pallas import tpu as pltpu

f32 = jnp.float32
bf16 = jnp.bfloat16

HEAD_DIM = 64
SB_HEADS = 4
SB_KV_HEADS = 2
MLA_HEADS = 4
MLA_Q_LORA = 256
MLA_KV_LORA = 128
MLA_NOPE = 64
MLA_ROPE = 32
MLA_V = 64
ROPE_BASE = 10000.0
NSA_HEADS = 8
CMP_STRIDE = 16
CMP_LEN = 2 * CMP_STRIDE
CMP_HID = 256
SEL_BLOCK = 64
CH_PER_SEL = SEL_BLOCK // CMP_STRIDE
SEL_TOPK = 16
WINDOW = 512
N_BUCKETS = 32
MAX_DISTANCE = 128
RMS_EPS = 1e-6
NEG_INF = -1e30
FORCE_SCORE = 1e4

LANES = 128
VMEM_LIMIT = 56 * 1024 * 1024

_COL = dict(nq=0, sbq=512, cq=768, sbk=1024, sbv=1152, ckv=1280, ckcv=1408, sksv=1536, svsk=1664,
            wkwv=1792, wvwk=1920, misc=2048, rot=2176)
NP = 2304
GATE_LANE = MLA_ROPE


def _cparams(sem):
    return pltpu.CompilerParams(dimension_semantics=sem, vmem_limit_bytes=VMEM_LIMIT)


def _rms(x, g):
    return x * lax.rsqrt(jnp.mean(x * x, axis=-1, keepdims=True) + RMS_EPS) * g


def _dot(a, b):
    return jnp.dot(a, b, preferred_element_type=f32)


def _dot_nt(a, b):
    return lax.dot_general(a, b, (((1,), (1,)), ((), ())), preferred_element_type=f32)


def _iota(shape, dim):
    return lax.broadcasted_iota(jnp.int32, shape, dim)


def _t5_thresholds():
    exact = N_BUCKETS // 2
    d = np.arange(0, 2 * MAX_DISTANCE + 1)
    ratio = np.log(np.maximum(d, 1).astype(np.float32) / exact) / math.log(MAX_DISTANCE / exact)
    large = np.minimum(exact + (ratio * (N_BUCKETS - exact)).astype(np.int32), N_BUCKETS - 1)
    bucket = np.where(d < exact, d, large)
    assert np.all(np.diff(bucket) >= 0)
    return [int(np.argmax(bucket >= b)) for b in range(1, N_BUCKETS)]


_T5_THR = _t5_thresholds()


def _bias_lookup(dist, rb_ref, h):
    val = jnp.full(dist.shape, rb_ref[h], f32)
    for b in range(1, N_BUCKETS):
        val = jnp.where(dist >= _T5_THR[b - 1], rb_ref[b * NSA_HEADS + h], val)
    return val


def _bias_tables_kernel(rb_ref, tsw_ref, tcmp_ref, *, wc):
    r = _iota((LANES, LANES), 0)
    c = _iota((LANES, LANES), 1)
    rc = _iota((LANES, wc), 0)
    cc = _iota((LANES, wc), 1)
    dcmp = jnp.maximum(rc - CMP_STRIDE * cc + (LANES - CMP_STRIDE + 1), 0)
    for h in range(NSA_HEADS):
        far = rb_ref[(N_BUCKETS - 1) * NSA_HEADS + h]
        d0 = r - c
        tsw_ref[h, 0] = jnp.where(d0 >= 0, _bias_lookup(jnp.maximum(d0, 0), rb_ref, h), NEG_INF)
        tsw_ref[h, 1] = _bias_lookup(d0 + LANES, rb_ref, h)
        tsw_ref[h, 2] = jnp.full((LANES, LANES), far, f32)
        tcmp_ref[h] = jnp.where(cc < 2 * LANES // CMP_STRIDE, _bias_lookup(dcmp, rb_ref, h), far)


def _bias_tables(rel_bias, wc):
    return pl.pallas_call(
        functools.partial(_bias_tables_kernel, wc=wc),
        out_shape=(jax.ShapeDtypeStruct((NSA_HEADS, 3, LANES, LANES), f32),
                   jax.ShapeDtypeStruct((NSA_HEADS, LANES, wc), f32)),
        in_specs=[pl.BlockSpec(memory_space=pltpu.SMEM)],
    )(rel_bias.reshape(-1))


def _proj_kernel(x_ref, g_ref, w_ref, of_ref, ob_ref):
    h = _rms(x_ref[...], g_ref[...]).astype(bf16)
    y = _dot(h, w_ref[...])
    of_ref[...] = y
    ob_ref[...] = y.astype(bf16)


def _proj(x, g, w):
    m, d = x.shape
    tm = min(256, m)
    return pl.pallas_call(
        _proj_kernel,
        grid=(m // tm,),
        in_specs=[pl.BlockSpec((tm, d), lambda i: (i, 0)),
                  pl.BlockSpec((1, d), lambda i: (0, 0)),
                  pl.BlockSpec((d, NP), lambda i: (0, 0))],
        out_specs=(pl.BlockSpec((tm, NP), lambda i: (i, 0)),
                   pl.BlockSpec((tm, NP), lambda i: (i, 0))),
        out_shape=(jax.ShapeDtypeStruct((m, NP), f32), jax.ShapeDtypeStruct((m, NP), bf16)),
        compiler_params=_cparams(("parallel",)),
    )(x, g.reshape(1, d), w)


def _outproj_kernel(oa_ref, ob_ref, oc_ref, wa_ref, wb_ref, wc_ref, g_ref, x_ref, o_ref):
    y = _dot(oa_ref[...], wa_ref[...]) + _dot(ob_ref[...], wb_ref[...]) + _dot(oc_ref[...], wc_ref[...])
    o_ref[...] = x_ref[...] + _rms(y, g_ref[...])


def _outproj(oa, ob, oc, wa, wb, wc, g, x):
    m, d = x.shape
    tm = min(512, m)
    row = lambda i: (i, 0)
    fixed = lambda i: (0, 0)
    return pl.pallas_call(
        _outproj_kernel,
        grid=(m // tm,),
        in_specs=[pl.BlockSpec((tm, oa.shape[1]), row), pl.BlockSpec((tm, ob.shape[1]), row),
                  pl.BlockSpec((tm, oc.shape[1]), row),
                  pl.BlockSpec(wa.shape, fixed), pl.BlockSpec(wb.shape, fixed), pl.BlockSpec(wc.shape, fixed),
                  pl.BlockSpec((1, d), fixed), pl.BlockSpec((tm, d), row)],
        out_specs=pl.BlockSpec((tm, d), row),
        out_shape=jax.ShapeDtypeStruct((m, d), f32),
        compiler_params=_cparams(("parallel",)),
    )(oa, ob, oc, wa, wb, wc, g.reshape(1, d), x)


def _ffn_kernel(x_ref, gpre_ref, wg_ref, wu_ref, wd_ref, gpost_ref, o_ref, hf_ref, acc_ref):
    j = pl.program_id(1)

    @pl.when(j == 0)
    def _():
        hf_ref[...] = _rms(x_ref[...], gpre_ref[...]).astype(bf16)
        acc_ref[...] = jnp.zeros_like(acc_ref)

    hf = hf_ref[...]
    a = _dot(hf, wg_ref[...])
    u = _dot(hf, wu_ref[...])
    act = (a / (1.0 + jnp.exp(-a))) * u
    acc_ref[...] += _dot(act.astype(bf16), wd_ref[...])

    @pl.when(j == pl.num_programs(1) - 1)
    def _():
        o_ref[...] = x_ref[...] + _rms(acc_ref[...], gpost_ref[...])


def _ffn(x, gpre, wg, wu, wd, gpost):
    m, d = x.shape
    ff = wg.shape[1]
    tm = min(512, m)
    tf = ff // 2 if (ff // 2) % LANES == 0 else ff
    return pl.pallas_call(
        _ffn_kernel,
        grid=(m // tm, ff // tf),
        in_specs=[pl.BlockSpec((tm, d), lambda i, j: (i, 0)),
                  pl.BlockSpec((1, d), lambda i, j: (0, 0)),
                  pl.BlockSpec((d, tf), lambda i, j: (0, j)),
                  pl.BlockSpec((d, tf), lambda i, j: (0, j)),
                  pl.BlockSpec((tf, d), lambda i, j: (j, 0)),
                  pl.BlockSpec((1, d), lambda i, j: (0, 0))],
        out_specs=pl.BlockSpec((tm, d), lambda i, j: (i, 0)),
        out_shape=jax.ShapeDtypeStruct((m, d), f32),
        scratch_shapes=[pltpu.VMEM((tm, d), bf16), pltpu.VMEM((tm, d), f32)],
        compiler_params=_cparams(("parallel", "arbitrary")),
    )(x, gpre.reshape(1, d), wg, wu, wd, gpost.reshape(1, d))


def _mla_prep_kernel(cq_ref, ckv_ref, misc_ref, rot_ref, gq_ref, gkv_ref, wq_ref, wuk_ref, cos_ref, sin_ref,
                     qcat_ref, kcat_ref, ckvn_ref, krn_ref):
    cos = cos_ref[...]
    sin = sin_ref[...]
    cq = _rms(cq_ref[...], gq_ref[...]).astype(bf16)
    qa = _dot(cq, wq_ref[...])
    nh = MLA_HEADS
    for h in range(nh):
        nope = qa[:, h * LANES:(h + 1) * LANES].astype(bf16)
        qlat = _dot(nope, wuk_ref[h])
        rope = qa[:, (nh + h) * LANES:(nh + h + 1) * LANES] * cos + qa[:, (2 * nh + h) * LANES:(2 * nh + h + 1) * LANES] * sin
        qcat_ref[:, 2 * h * LANES:(2 * h + 1) * LANES] = qlat.astype(bf16)
        qcat_ref[:, (2 * h + 1) * LANES:(2 * h + 2) * LANES] = rope.astype(bf16)
    ckv = _rms(ckv_ref[...], gkv_ref[...])
    kr = misc_ref[...] * cos + rot_ref[...] * sin
    ckvn_ref[...] = ckv
    krn_ref[...] = kr
    kcat_ref[:, :LANES] = ckv.astype(bf16)
    lane = _iota(kr.shape, 1)
    kcat_ref[:, LANES:] = jnp.where(lane < MLA_ROPE, kr, 1.0).astype(bf16)


def _mla_prep(pf, gq, gkv, wq, wuk, cos_t, sin_t, n_pos_blocks):
    m = pf.shape[0]
    tm = min(256, m)
    cb = lambda name, w: _COL[name] // w
    tab = lambda i: (i % n_pos_blocks, 0)
    fixed2 = lambda i: (0, 0)
    return pl.pallas_call(
        _mla_prep_kernel,
        grid=(m // tm,),
        in_specs=[pl.BlockSpec((tm, MLA_Q_LORA), lambda i: (i, cb("cq", MLA_Q_LORA))),
                  pl.BlockSpec((tm, LANES), lambda i: (i, cb("ckv", LANES))),
                  pl.BlockSpec((tm, LANES), lambda i: (i, cb("misc", LANES))),
                  pl.BlockSpec((tm, LANES), lambda i: (i, cb("rot", LANES))),
                  pl.BlockSpec((1, MLA_Q_LORA), fixed2), pl.BlockSpec((1, MLA_KV_LORA), fixed2),
                  pl.BlockSpec(wq.shape, fixed2), pl.BlockSpec(wuk.shape, lambda i: (0, 0, 0)),
                  pl.BlockSpec((tm, LANES), tab), pl.BlockSpec((tm, LANES), tab)],
        out_specs=(pl.BlockSpec((tm, 2 * LANES * MLA_HEADS), lambda i: (i, 0)),
                   pl.BlockSpec((tm, 2 * LANES), lambda i: (i, 0)),
                   pl.BlockSpec((tm, LANES), lambda i: (i, 0)),
                   pl.BlockSpec((tm, LANES), lambda i: (i, 0))),
        out_shape=(jax.ShapeDtypeStruct((m, 2 * LANES * MLA_HEADS), bf16),
                   jax.ShapeDtypeStruct((m, 2 * LANES), bf16),
                   jax.ShapeDtypeStruct((m, LANES), f32),
                   jax.ShapeDtypeStruct((m, LANES), f32)),
        compiler_params=_cparams(("parallel",)),
    )(pf, pf, pf, pf, gq.reshape(1, -1), gkv.reshape(1, -1), wq, wuk, cos_t, sin_t)


def _compress_math(chk, chv, pek_ref, pev_ref, w1k_ref, w1v_ref, w2kl_ref, w2kr_ref, w2vl_ref, w2vr_ref):
    nch = chk.shape[0]

    def hidden(ch, pe_ref, w1_ref):
        first = _dot((ch + pe_ref[0:1, :]).astype(bf16), w1_ref[0])
        second = _dot((ch + pe_ref[1:2, :]).astype(bf16), w1_ref[1])
        pre = first + pltpu.roll(second, nch - 1, axis=0)
        return (pre / (1.0 + jnp.exp(-pre))).astype(bf16)

    hk = hidden(chk, pek_ref, w1k_ref)
    hv = hidden(chv, pev_ref, w1v_ref)
    kv = _dot(hk, w2kl_ref[...]) + _dot(hv, w2vr_ref[...])
    vk = _dot(hv, w2vl_ref[...]) + _dot(hk, w2kr_ref[...])
    return kv, vk


def _compress_kernel(chk_ref, chv_ref, pek_ref, pev_ref, w1k_ref, w1v_ref, w2kl_ref, w2kr_ref, w2vl_ref, w2vr_ref,
                     kv_ref, vk_ref):
    kv, vk = _compress_math(chk_ref[0], chv_ref[0], pek_ref, pev_ref, w1k_ref, w1v_ref,
                            w2kl_ref, w2kr_ref, w2vl_ref, w2vr_ref)
    kv_ref[0] = kv.astype(bf16)
    vk_ref[0] = vk.astype(bf16)


def _compress_specs(cw):
    c2 = lambda *_: (0, 0)
    c3 = lambda *_: (0, 0, 0)
    return [pl.BlockSpec(cw["pek"].shape, c2), pl.BlockSpec(cw["pev"].shape, c2),
            pl.BlockSpec(cw["w1k"].shape, c3), pl.BlockSpec(cw["w1v"].shape, c3),
            pl.BlockSpec(cw["w2kl"].shape, c2), pl.BlockSpec(cw["w2kr"].shape, c2),
            pl.BlockSpec(cw["w2vl"].shape, c2), pl.BlockSpec(cw["w2vr"].shape, c2)]


def _compress_args(cw):
    return (cw["pek"], cw["pev"], cw["w1k"], cw["w1v"], cw["w2kl"], cw["w2kr"], cw["w2vl"], cw["w2vr"])


def _compress(chk, chv, cw):
    b, nch, cl = chk.shape
    blk = pl.BlockSpec((1, nch, cl), lambda i: (i, 0, 0))
    out = pl.BlockSpec((1, nch, LANES), lambda i: (i, 0, 0))
    return pl.pallas_call(
        _compress_kernel,
        grid=(b,),
        in_specs=[blk, blk] + _compress_specs(cw),
        out_specs=(out, out),
        out_shape=(jax.ShapeDtypeStruct((b, nch, LANES), bf16),) * 2,
        compiler_params=_cparams(("parallel",)),
    )(chk, chv, *_compress_args(cw))


def _softplus(z):
    return jnp.maximum(z, 0.0) + jnp.log1p(jnp.exp(-jnp.abs(z)))


def _sb_tile(qm, k_t, v_t, valid, carry, tri, ones):
    z = _dot_nt(qm, k_t) * (HEAD_DIM ** -0.5)
    lr = jnp.where(valid, -_softplus(z), 0.0)
    hi = lr.astype(bf16)
    lo = (lr - hi.astype(f32)).astype(bf16)
    after_in = _dot(hi, tri) + _dot(lo, tri)
    tot = _dot(hi, ones) + _dot(lo, ones)
    reps = z.shape[1] // LANES
    after = after_in + (jnp.concatenate([carry] * reps, axis=1) if reps > 1 else carry)
    w = jnp.where(valid, jnp.exp(z + lr + after), 0.0)
    return _dot(w.astype(bf16), v_t), tot


def _sb_prompt_kernel(q_ref, k_ref, v_ref, o_ref, acc_ref, car_ref, *, tq, tk):
    i = pl.program_id(1)
    lane = _iota((tq, LANES), 1)
    rk = _iota((tk, tk), 0)
    ck = _iota((tk, tk), 1)
    tri = jnp.where(rk > ck, 1.0, 0.0).astype(bf16)
    ones = jnp.ones((tk, LANES), bf16)
    qpos = i * tq + _iota((tq, tk), 0)
    kcol = _iota((tq, tk), 1)
    acc_ref[...] = jnp.zeros_like(acc_ref)
    car_ref[...] = jnp.zeros_like(car_ref)
    heads = [(g, h) for g in range(2) for h in range(SB_KV_HEADS)]
    qms = []
    for g, h in heads:
        qp = q_ref[:, g * LANES:(g + 1) * LANES]
        qms.append(jnp.where((lane // HEAD_DIM) == h, qp, jnp.zeros_like(qp)))
    nkb = (i * tq + tq - 1) // tk + 1

    def body(t, c):
        kb = nkb - 1 - t
        k_t = k_ref[pl.ds(pl.multiple_of(kb * tk, tk), tk), :]
        v_t = v_ref[pl.ds(pl.multiple_of(kb * tk, tk), tk), :]
        valid = (kb * tk + kcol) < qpos
        for n in range(len(heads)):
            pv, tot = _sb_tile(qms[n], k_t, v_t, valid, car_ref[n], tri, ones)
            acc_ref[n] += pv
            car_ref[n] += tot
        return c

    lax.fori_loop(0, nkb, body, 0)
    for g in range(2):
        o_ref[:, g * LANES:(g + 1) * LANES] = jnp.where(lane < HEAD_DIM, acc_ref[2 * g], acc_ref[2 * g + 1]).astype(bf16)


def _sb_prompt(pb, b, t):
    tq = min(256, t)
    tk = tq
    nq = t // tq
    return pl.pallas_call(
        functools.partial(_sb_prompt_kernel, tq=tq, tk=tk),
        grid=(b, nq),
        in_specs=[pl.BlockSpec((tq, 2 * LANES), lambda bb, i: (bb * nq + i, _COL["sbq"] // (2 * LANES))),
                  pl.BlockSpec((t, LANES), lambda bb, i: (bb, _COL["sbk"] // LANES)),
                  pl.BlockSpec((t, LANES), lambda bb, i: (bb, _COL["sbv"] // LANES))],
        out_specs=pl.BlockSpec((tq, 2 * LANES), lambda bb, i: (bb * nq + i, 0)),
        out_shape=jax.ShapeDtypeStruct((b * t, 2 * LANES), bf16),
        scratch_shapes=[pltpu.VMEM((SB_HEADS, tq, LANES), f32), pltpu.VMEM((SB_HEADS, tq, LANES), f32)],
        compiler_params=_cparams(("parallel", "parallel")),
    )(pb, pb, pb)


def _mla_prompt_kernel(q_ref, k_ref, wuv_ref, o_ref, m_ref, acc_ref, *, tq, tk):
    i = pl.program_id(1)
    nh = MLA_HEADS
    qs = jnp.concatenate([q_ref[:, h * 2 * LANES:(h + 1) * 2 * LANES] for h in range(nh)], axis=0)
    scale = (MLA_NOPE + MLA_ROPE) ** -0.5
    qpos = i * tq + (_iota((nh * tq, tk), 0) % tq)
    kcol = _iota((nh * tq, tk), 1)
    m_ref[...] = jnp.full(m_ref.shape, NEG_INF, f32)
    acc_ref[...] = jnp.zeros_like(acc_ref)
    nkb = (i * tq + tq - 1) // tk + 1

    def body(kb, c):
        k_t = k_ref[pl.ds(pl.multiple_of(kb * tk, tk), tk), :]
        s = _dot_nt(qs, k_t) * scale
        s = jnp.where((kb * tk + kcol) <= qpos, s, NEG_INF)
        m_old = m_ref[...]
        m_new = jnp.maximum(m_old, jnp.max(s, axis=-1, keepdims=True))
        p = jnp.exp(s - m_new)
        acc_ref[...] = jnp.exp(m_old - m_new) * acc_ref[...] + _dot(p.astype(bf16), k_t)
        m_ref[...] = m_new
        return c

    lax.fori_loop(0, nkb, body, 0)
    acc = acc_ref[...]
    olat = (acc[:, :LANES] / acc[:, 2 * LANES - 1:2 * LANES]).astype(bf16)
    out = _dot(olat[0:tq], wuv_ref[0])
    for h in range(1, nh):
        out += _dot(olat[h * tq:(h + 1) * tq], wuv_ref[h])
    o_ref[...] = out.astype(bf16)


def _mla_prompt(qcat, kcat, wuv, b, t):
    tq = min(128, t)
    tk = min(512, t)
    nq = t // tq
    return pl.pallas_call(
        functools.partial(_mla_prompt_kernel, tq=tq, tk=tk),
        grid=(b, nq),
        in_specs=[pl.BlockSpec((tq, qcat.shape[1]), lambda bb, i: (bb * nq + i, 0)),
                  pl.BlockSpec((t, 2 * LANES), lambda bb, i: (bb, 0)),
                  pl.BlockSpec(wuv.shape, lambda bb, i: (0, 0, 0))],
        out_specs=pl.BlockSpec((tq, MLA_HEADS * MLA_V), lambda bb, i: (bb * nq + i, 0)),
        out_shape=jax.ShapeDtypeStruct((b * t, MLA_HEADS * MLA_V), bf16),
        scratch_shapes=[pltpu.VMEM((MLA_HEADS * tq, 1), f32), pltpu.VMEM((MLA_HEADS * tq, 2 * LANES), f32)],
        compiler_params=_cparams(("parallel", "parallel")),
    )(qcat, kcat, wuv)


def _split3(x):
    hi = x.astype(bf16)
    r1 = x - hi.astype(f32)
    mid = r1.astype(bf16)
    lo = (r1 - mid.astype(f32)).astype(bf16)
    return hi, mid, lo


def _online_update(s, mask, v_t, m_ref, acc_ref, idx):
    sm = s if mask is None else jnp.where(mask, s, NEG_INF)
    m_old = m_ref[idx]
    m_new = jnp.maximum(m_old, jnp.max(sm, axis=-1, keepdims=True))
    p = jnp.exp(s - m_new)
    if mask is not None:
        p = jnp.where(mask, p, 0.0)
    acc_ref[idx] = jnp.exp(m_old - m_new) * acc_ref[idx] + _dot(p.astype(bf16), v_t)
    m_ref[idx] = m_new


def _pair_outputs(acc_ref, gates, branch, lane_lo):
    outs = []
    for p in range(NSA_HEADS // 2):
        pair = []
        for h in (2 * p, 2 * p + 1):
            a = acc_ref[h]
            o = a * pltpu.roll(1.0 / a, HEAD_DIM, axis=1)
            gl = GATE_LANE + branch * NSA_HEADS + h
            pair.append(o * gates[:, gl:gl + 1])
        outs.append(jnp.where(lane_lo, pair[0], pair[1]))
    return outs


def _nsa_prompt_kernel(q_ref, misc_ref, kvs_ref, vks_ref, kvw_ref, vkw_ref, kvc_ref, vkc_ref,
                       tsw_ref, tcmp_ref, amat_ref, o_ref, m_ref, acc_ref, *, tq, n_sel, k_top):
    i = pl.program_id(1)
    nh = NSA_HEADS
    npair = nh // 2
    wc = kvc_ref.shape[1]
    scale = HEAD_DIM ** -0.5
    lane = _iota((tq, LANES), 1)
    lane_lo = lane < HEAD_DIM
    row = _iota((tq, LANES), 0)
    qpos = i * tq + row

    qe, qo = [], []
    for p in range(npair):
        qp = q_ref[:, p * LANES:(p + 1) * LANES]
        qe.append(jnp.where(lane_lo, qp, jnp.zeros_like(qp)))
        qo.append(jnp.where(lane_lo, jnp.zeros_like(qp), qp))
    qe = jnp.concatenate(qe, axis=0)
    qo = jnp.concatenate(qo, axis=0)

    def head_rows(se, so, h):
        src = se if h % 2 == 0 else so
        return src[(h // 2) * tq:(h // 2 + 1) * tq]

    gates = 1.0 / (1.0 + jnp.exp(-misc_ref[...]))

    kvc = kvc_ref[0]
    vkc = vkc_ref[0]
    se = _dot_nt(qe, kvc) * scale
    so = _dot_nt(qo, vkc) * scale
    ncol = _iota((tq, wc), 1)
    qpos_c = i * tq + _iota((tq, wc), 0)
    okc = (CMP_STRIDE * ncol + (CMP_LEN - 1)) <= qpos_c
    shift = (i * (tq // CMP_STRIDE) + wc - (LANES // CMP_STRIDE + 1)) % wc
    imp = jnp.zeros((tq, wc), f32)
    oc = []
    for h in range(nh):
        s = head_rows(se, so, h) + pltpu.roll(tcmp_ref[h], shift, axis=1)
        sm = jnp.where(okc, s, NEG_INF)
        mx = jnp.max(sm, axis=-1, keepdims=True)
        p = jnp.where(okc, jnp.exp(s - mx), 0.0)
        l = jnp.sum(p, axis=-1, keepdims=True)
        pc = p * jnp.where(l > 0.0, 1.0 / l, 0.0)
        imp = imp + pc
        oc.append(_dot(pc.astype(bf16), vkc if h % 2 == 0 else kvc))
    out = []
    for p in range(npair):
        ge = gates[:, GATE_LANE + 2 * p:GATE_LANE + 2 * p + 1]
        go = gates[:, GATE_LANE + 2 * p + 1:GATE_LANE + 2 * p + 2]
        out.append(jnp.where(lane_lo, oc[2 * p] * ge, oc[2 * p + 1] * go))

    amat = amat_ref[...]
    hi, mid, lo = _split3(imp)
    p_slc = _dot(hi, amat) + _dot(mid, amat) + _dot(lo, amat)
    cur = qpos // SEL_BLOCK
    forced = (lane == 0) | (lane == cur) | (lane == cur - 1)
    score = jnp.where(lane <= cur, jnp.where(forced, FORCE_SCORE, p_slc), -1.0)
    st = score.T[0:n_sel]
    jrow = _iota((n_sel, tq), 0)
    rank = jnp.zeros((n_sel, tq), f32)
    for b in range(n_sel):
        rb = st[b:b + 1, :]
        rank = rank + jnp.where(rb > st, 1.0, jnp.where((rb == st) & (jrow > b), 1.0, 0.0))
    sel_t = jnp.where((rank < k_top) & (st >= 0.0), 1.0, 0.0)
    if n_sel < LANES:
        sel_t = jnp.concatenate([sel_t, jnp.zeros((LANES - n_sel, tq), f32)], axis=0)
    sel = sel_t.T.astype(bf16)

    def reset():
        m_ref[...] = jnp.full(m_ref.shape, NEG_INF, f32)
        acc_ref[...] = jnp.zeros_like(acc_ref)

    jblk = _iota((LANES, LANES), 0)
    cblk = _iota((LANES, LANES), 1) // SEL_BLOCK
    one_b = jnp.ones((), bf16)

    def attend(kv_ref, vk_ref, kb, mask):
        off = pl.multiple_of(kb * LANES, LANES)
        kv = kv_ref[pl.ds(off, LANES), :]
        vk = vk_ref[pl.ds(off, LANES), :]
        se = _dot_nt(qe, kv) * scale
        so = _dot_nt(qo, vk) * scale
        lk = _iota(kv.shape, 1) < HEAD_DIM
        v_even = jnp.where(lk, vk, one_b)
        v_odd = jnp.where(lk, one_b, kv)
        d = jnp.minimum(i - kb, 2)
        for h in range(nh):
            s = head_rows(se, so, h) + tsw_ref[h, d]
            _online_update(s, mask, v_even if h % 2 == 0 else v_odd, m_ref, acc_ref, h)

    reset()

    def sel_body(kb, c):
        emat = jnp.where(jblk == (LANES // SEL_BLOCK) * kb + cblk, 1.0, 0.0).astype(bf16)
        mask = _dot(sel, emat) > 0.5
        attend(kvs_ref, vks_ref, kb, mask)
        return c

    lax.fori_loop(0, i + 1, sel_body, 0)
    for p, blk in enumerate(_pair_outputs(acc_ref, gates, 1, lane_lo)):
        out[p] = out[p] + blk

    reset()
    kcol = _iota((tq, LANES), 1)

    def win_body(kb, c):
        kpos = kb * LANES + kcol
        mask = (kpos <= qpos) & (kpos > qpos - WINDOW)
        attend(kvw_ref, vkw_ref, kb, mask)
        return c

    lax.fori_loop(jnp.maximum(i - WINDOW // LANES, 0), i + 1, win_body, 0)
    for p, blk in enumerate(_pair_outputs(acc_ref, gates, 2, lane_lo)):
        out[p] = out[p] + blk
    for p in range(npair):
        o_ref[:, p * LANES:(p + 1) * LANES] = out[p].astype(bf16)


def _sel_matrix(wc, n_cmp):
    a = np.zeros((wc, LANES), np.float32)
    for n in range(n_cmp):
        a[n, n // CH_PER_SEL] += 0.5
        a[n, (n + 1) // CH_PER_SEL] += 0.5
    return a


def _nsa_prompt(pb, pf, kvc, vkc, tsw, tcmp, b, t):
    tq = LANES
    nq = t // tq
    wc = kvc.shape[1]
    n_sel = t // SEL_BLOCK
    amat = jnp.asarray(_sel_matrix(wc, wc - 1), bf16)
    full = lambda name: pl.BlockSpec((t, LANES), lambda bb, i, n=name: (bb, _COL[n] // LANES))
    return pl.pallas_call(
        functools.partial(_nsa_prompt_kernel, tq=tq, n_sel=n_sel, k_top=min(SEL_TOPK, n_sel)),
        grid=(b, nq),
        in_specs=[pl.BlockSpec((tq, NSA_HEADS * HEAD_DIM), lambda bb, i: (bb * nq + i, 0)),
                  pl.BlockSpec((tq, LANES), lambda bb, i: (bb * nq + i, _COL["misc"] // LANES)),
                  full("sksv"), full("svsk"), full("wkwv"), full("wvwk"),
                  pl.BlockSpec((1, wc, LANES), lambda bb, i: (bb, 0, 0)),
                  pl.BlockSpec((1, wc, LANES), lambda bb, i: (bb, 0, 0)),
                  pl.BlockSpec(tsw.shape, lambda bb, i: (0, 0, 0, 0)),
                  pl.BlockSpec(tcmp.shape, lambda bb, i: (0, 0, 0)),
                  pl.BlockSpec(amat.shape, lambda bb, i: (0, 0))],
        out_specs=pl.BlockSpec((tq, NSA_HEADS * HEAD_DIM), lambda bb, i: (bb * nq + i, 0)),
        out_shape=jax.ShapeDtypeStruct((b * t, NSA_HEADS * HEAD_DIM), bf16),
        scratch_shapes=[pltpu.VMEM((NSA_HEADS, tq, 1), f32), pltpu.VMEM((NSA_HEADS, tq, LANES), f32)],
        compiler_params=_cparams(("parallel", "parallel")),
    )(pb, pf, pb, pb, pb, pb, kvc, vkc, tsw, tcmp, amat)


_IN_SIZES = (SB_HEADS * HEAD_DIM, SB_KV_HEADS * HEAD_DIM, SB_KV_HEADS * HEAD_DIM,
             MLA_Q_LORA, MLA_KV_LORA, MLA_ROPE,
             NSA_HEADS * HEAD_DIM, HEAD_DIM, HEAD_DIM, HEAD_DIM, HEAD_DIM, HEAD_DIM, HEAD_DIM,
             3 * NSA_HEADS)
_SB_HEAD_ORDER = (0, 2, 1, 3)


def _rot_half_cols(w):
    half = w.shape[-1] // 2
    return jnp.concatenate([-w[..., half:], w[..., :half]], axis=-1)


def _prep_w_in(w):
    offs = np.concatenate([[0], np.cumsum(_IN_SIZES)])
    (sbq, sbk, sbv, cq, ckv, kr, nq, ck, cv, sk, sv, wk, wv, gl) = [w[:, offs[n]:offs[n + 1]] for n in range(len(_IN_SIZES))]
    sbq = jnp.concatenate([sbq[:, hh * HEAD_DIM:(hh + 1) * HEAD_DIM] for hh in _SB_HEAD_ORDER], axis=1)
    z = lambda n: jnp.zeros((w.shape[0], n), w.dtype)
    cols = [nq, sbq, cq, sbk, sbv, ckv, ck, cv, sk, sv, sv, sk, wk, wv, wv, wk,
            kr, gl, z(LANES - MLA_ROPE - 3 * NSA_HEADS), _rot_half_cols(kr), z(LANES - MLA_ROPE)]
    out = jnp.concatenate(cols, axis=1)
    assert out.shape[1] == NP
    return out.astype(bf16)


def _prep_w_uq(w_uq):
    d = w_uq.shape[0]
    per = w_uq.reshape(d, MLA_HEADS, MLA_NOPE + MLA_ROPE)
    nope = jnp.pad(per[:, :, :MLA_NOPE], ((0, 0), (0, 0), (0, LANES - MLA_NOPE)))
    rope = per[:, :, MLA_NOPE:]
    rpad = ((0, 0), (0, 0), (0, LANES - MLA_ROPE))
    blocks = [nope, jnp.pad(rope, rpad), jnp.pad(_rot_half_cols(rope), rpad)]
    return jnp.concatenate([x.reshape(d, MLA_HEADS * LANES) for x in blocks], axis=1).astype(bf16)


def _prep_w_uk(w_uk):
    t = w_uk.reshape(MLA_KV_LORA, MLA_HEADS, MLA_NOPE).transpose(1, 2, 0)
    return jnp.pad(t, ((0, 0), (0, LANES - MLA_NOPE), (0, 0))).astype(bf16)


def _prep_w_uv(w_uv):
    per = w_uv.reshape(MLA_KV_LORA, MLA_HEADS, MLA_V)
    out = jnp.zeros((MLA_HEADS, MLA_KV_LORA, MLA_HEADS * MLA_V), w_uv.dtype)
    for h in range(MLA_HEADS):
        out = out.at[h, :, h * MLA_V:(h + 1) * MLA_V].set(per[:, h, :])
    return out.astype(bf16)


def _prep_compress(pos_k, w1_k, w2_k, pos_v, w1_v, w2_v):
    def w1(w):
        return w.reshape(2, CMP_STRIDE * HEAD_DIM, CMP_HID).astype(bf16)

    def pe(p):
        return p.reshape(2, CMP_STRIDE * HEAD_DIM)

    zl = jnp.zeros_like(w2_k)
    return dict(pek=pe(pos_k), pev=pe(pos_v), w1k=w1(w1_k), w1v=w1(w1_v),
                w2kl=jnp.concatenate([w2_k, zl], 1).astype(bf16), w2kr=jnp.concatenate([zl, w2_k], 1).astype(bf16),
                w2vl=jnp.concatenate([w2_v, zl], 1).astype(bf16), w2vr=jnp.concatenate([zl, w2_v], 1).astype(bf16))


def _prep_w_out(w):
    na = SB_HEADS * HEAD_DIM
    nb = MLA_HEADS * MLA_V
    wa = jnp.concatenate([w[hh * HEAD_DIM:(hh + 1) * HEAD_DIM] for hh in _SB_HEAD_ORDER], axis=0)
    return wa.astype(bf16), w[na:na + nb].astype(bf16), w[na + nb:].astype(bf16)


def _rope_tables(pos):
    half = MLA_ROPE // 2
    inv = ROPE_BASE ** (-jnp.arange(half, dtype=f32) / half)
    ang = pos.astype(f32)[:, None] * inv[None, :]
    pad = ((0, 0), (0, LANES - MLA_ROPE))
    cos = jnp.cos(ang)
    sin = jnp.sin(ang)
    return jnp.pad(jnp.concatenate([cos, cos], 1), pad), jnp.pad(jnp.concatenate([sin, sin], 1), pad)


def _layer_weights(W, l):
    return dict(
        w_in=_prep_w_in(W["w_in"][l]), wq=_prep_w_uq(W["w_uq"][l]), wuk=_prep_w_uk(W["w_uk"][l]),
        wuv=_prep_w_uv(W["w_uv"][l]), gq=W["g_q_lora"][l], gkv=W["g_kv_lora"][l],
        cw=_prep_compress(W["cmp_pos_k"][l], W["cmp_w1_k"][l], W["cmp_w2_k"][l],
                          W["cmp_pos_v"][l], W["cmp_w1_v"][l], W["cmp_w2_v"][l]),
        w_out=_prep_w_out(W["w_out"][l]),
        g_attn_pre=W["g_attn_pre"][l], g_attn_post=W["g_attn_post"][l],
        g_ffn_pre=W["g_ffn_pre"][l], g_ffn_post=W["g_ffn_post"][l],
        w_gate=W["w_gate"][l].astype(bf16), w_up=W["w_up"][l].astype(bf16), w_down=W["w_down"][l].astype(bf16))


def _cache_rows(pf, ckvn, krn):
    c = _COL
    hd = HEAD_DIM
    return dict(sb_k=pf[:, c["sbk"]:c["sbk"] + LANES], sb_v=pf[:, c["sbv"]:c["sbv"] + LANES],
                ckv=ckvn, kr=krn[:, :MLA_ROPE],
                ck=pf[:, c["ckcv"]:c["ckcv"] + hd], cv=pf[:, c["ckcv"] + hd:c["ckcv"] + 2 * hd],
                sk=pf[:, c["sksv"]:c["sksv"] + hd], sv=pf[:, c["sksv"] + hd:c["sksv"] + 2 * hd],
                wk=pf[:, c["wkwv"]:c["wkwv"] + hd], wv=pf[:, c["wkwv"] + hd:c["wkwv"] + 2 * hd])


def _prompt_layer(x, lw, b, t, tabs):
    cos_t, sin_t, tsw, tcmp = tabs
    pf, pb = _proj(x, lw["g_attn_pre"], lw["w_in"])
    qcat, kcat, ckvn, krn = _mla_prep(pf, lw["gq"], lw["gkv"], lw["wq"], lw["wuk"], cos_t, sin_t,
                                      t // min(256, b * t))
    rows = _cache_rows(pf, ckvn, krn)
    nch = t // CMP_STRIDE
    kvc, vkc = _compress(rows["ck"].reshape(b, nch, CMP_STRIDE * HEAD_DIM),
                         rows["cv"].reshape(b, nch, CMP_STRIDE * HEAD_DIM), lw["cw"])
    oa = _sb_prompt(pb, b, t)
    ob = _mla_prompt(qcat, kcat, lw["wuv"], b, t)
    oc = _nsa_prompt(pb, pf, kvc, vkc, tsw, tcmp, b, t)
    wa, wb, wc = lw["w_out"]
    x = _outproj(oa, ob, oc, wa, wb, wc, lw["g_attn_post"], x)
    x = _ffn(x, lw["g_ffn_pre"], lw["w_gate"], lw["w_up"], lw["w_down"], lw["g_ffn_post"])
    return x, rows


def _page_copy(cache_ref, buf_ref, sem_ref, page, slot, p):
    rows = cache_ref.shape[1]
    return pltpu.make_async_copy(cache_ref.at[page], buf_ref.at[slot, pl.ds(p * rows, rows)], sem_ref.at[slot])


def _gather_pages(pt_ref, caches, bufs, sems, base, npg):
    s = pl.program_id(0)
    slot = s % 2

    def start(seq, sl):
        def body(p, c):
            page = base + pt_ref[seq * npg + p]
            for cache, buf, sem in zip(caches, bufs, sems):
                _page_copy(cache, buf, sem, page, sl, p).start()
            return c
        lax.fori_loop(0, npg, body, 0)

    @pl.when(s == 0)
    def _():
        start(0, 0)

    @pl.when(s + 1 < pl.num_programs(0))
    def _():
        start(s + 1, 1 - slot)

    def wait_body(p, c):
        for cache, buf, sem in zip(caches, bufs, sems):
            _page_copy(cache, buf, sem, 0, slot, p).wait()
        return c
    lax.fori_loop(0, npg, wait_body, 0)
    return slot


def _decode_bias_kernel(rb_ref, tsel_ref, twin_ref, tcmp_ref, tnew_ref, *, past):
    def rowdist(width, fn):
        return jnp.maximum(fn(_iota((1, width), 1)), 0)

    dsel = rowdist(tsel_ref.shape[1], lambda c: tsel_ref.shape[1] - c)
    dwin = rowdist(twin_ref.shape[1], lambda c: twin_ref.shape[1] - c)
    dcmp = rowdist(tcmp_ref.shape[1], lambda n: past - (CMP_STRIDE * n + CMP_LEN - 1))
    lane = _iota((1, LANES), 1)
    for h in range(NSA_HEADS):
        tsel_ref[h:h + 1, :] = _bias_lookup(dsel, rb_ref, h)
        twin_ref[h:h + 1, :] = _bias_lookup(dwin, rb_ref, h)
        tcmp_ref[h:h + 1, :] = _bias_lookup(dcmp, rb_ref, h)
        tnew_ref[h:h + 1, :] = jnp.where(lane < HEAD_DIM, rb_ref[h], rb_ref[(N_BUCKETS - 1) * NSA_HEADS + h])


def _decode_bias_tables(rel_bias, past, ck_sel, pw, nch):
    shp = lambda w: jax.ShapeDtypeStruct((NSA_HEADS, w), f32)
    return pl.pallas_call(
        functools.partial(_decode_bias_kernel, past=past),
        out_shape=(shp(ck_sel), shp(pw), shp(nch), shp(LANES)),
        in_specs=[pl.BlockSpec(memory_space=pltpu.SMEM)],
    )(rel_bias.reshape(-1))


def _sb_decode_kernel(pt_ref, q_ref, kc_ref, vc_ref, o_ref, kbuf, vbuf, sems, *, base, npg, ck):
    slot = _gather_pages(pt_ref, (kc_ref, vc_ref), (kbuf, vbuf), (sems.at[0], sems.at[1]), base, npg)
    past = kbuf.shape[1]
    qg = q_ref[0]
    lane = _iota(qg.shape, 1)
    zero = jnp.zeros_like(qg)
    qm = jnp.concatenate([jnp.where(lane < HEAD_DIM, qg, zero), jnp.where(lane < HEAD_DIM, zero, qg),
                          jnp.zeros((4, LANES), f32)], axis=0).astype(bf16)
    tri = jnp.where(_iota((ck, ck), 0) > _iota((ck, ck), 1), 1.0, 0.0).astype(bf16)
    scale = HEAD_DIM ** -0.5
    nchunk = past // ck

    def body(t, carry):
        acc, run = carry
        off = pl.multiple_of((nchunk - 1 - t) * ck, ck)
        k_c = kbuf[slot, pl.ds(off, ck), :].astype(bf16)
        v_c = vbuf[slot, pl.ds(off, ck), :].astype(bf16)
        z = _dot_nt(qm, k_c) * scale
        lr = -_softplus(z)
        hi = lr.astype(bf16)
        lo = (lr - hi.astype(f32)).astype(bf16)
        after = _dot(hi, tri) + _dot(lo, tri) + run
        w = jnp.exp(z + lr + after)
        return acc + _dot(w.astype(bf16), v_c), run + jnp.sum(lr, axis=-1, keepdims=True)

    acc, _ = lax.fori_loop(0, nchunk, body, (jnp.zeros((8, LANES), f32), jnp.zeros((8, 1), f32)))
    o_ref[0] = jnp.where(lane < HEAD_DIM, acc[0:2], acc[2:4])


def _sb_decode(q, cache_k, cache_v, pt, layer):
    s = q.shape[0]
    depth, n_pool, page, _, _ = cache_k.shape
    npg = pt.shape[1]
    past = npg * page
    ck = min(512, past)
    ck2 = lambda c: c.reshape(depth * n_pool, page, LANES)
    grid_spec = pltpu.PrefetchScalarGridSpec(
        num_scalar_prefetch=1, grid=(s,),
        in_specs=[pl.BlockSpec((1, 2, LANES), lambda i, pt_: (i, 0, 0)),
                  pl.BlockSpec(memory_space=pl.ANY), pl.BlockSpec(memory_space=pl.ANY)],
        out_specs=pl.BlockSpec((1, 2, LANES), lambda i, pt_: (i, 0, 0)),
        scratch_shapes=[pltpu.VMEM((2, past, LANES), f32), pltpu.VMEM((2, past, LANES), f32),
                        pltpu.SemaphoreType.DMA((2, 2))])
    out = pl.pallas_call(
        functools.partial(_sb_decode_kernel, base=layer * n_pool, npg=npg, ck=ck),
        grid_spec=grid_spec,
        out_shape=jax.ShapeDtypeStruct((s, 2, LANES), f32),
        compiler_params=_cparams(("arbitrary",)),
    )(pt.reshape(-1), q.reshape(s, 2, LANES), ck2(cache_k), ck2(cache_v))
    return out.reshape(s, 2 * LANES).astype(bf16)


def _mla_decode_kernel(pt_ref, q_ref, kn_ref, wuv_ref, cc_ref, rc_ref, o_ref, cbuf, rbuf, sems, *, base, npg, ck):
    slot = _gather_pages(pt_ref, (cc_ref, rc_ref), (cbuf, rbuf), (sems.at[0], sems.at[1]), base, npg)
    past = cbuf.shape[1]
    nh = MLA_HEADS
    q = jnp.concatenate([q_ref[0], jnp.zeros((8 - nh, 2 * LANES), f32)], axis=0)
    qlat = q[:, :LANES].astype(bf16)
    qrope = q[:, LANES:LANES + MLA_ROPE].astype(bf16)
    scale = (MLA_NOPE + MLA_ROPE) ** -0.5

    def body(t, carry):
        m, l, acc = carry
        off = pl.multiple_of(t * ck, ck)
        c_c = cbuf[slot, pl.ds(off, ck), :].astype(bf16)
        r_c = rbuf[slot, pl.ds(off, ck), :].astype(bf16)
        s = (_dot_nt(qlat, c_c) + _dot_nt(qrope, r_c)) * scale
        m_new = jnp.maximum(m, jnp.max(s, axis=-1, keepdims=True))
        p = jnp.exp(s - m_new)
        a = jnp.exp(m - m_new)
        return m_new, a * l + jnp.sum(p, axis=-1, keepdims=True), a * acc + _dot(p.astype(bf16), c_c)

    init = (jnp.full((8, 1), NEG_INF, f32), jnp.zeros((8, 1), f32), jnp.zeros((8, LANES), f32))
    m, l, acc = lax.fori_loop(0, past // ck, body, init)
    kn = kn_ref[0]
    s_new = jnp.sum(q * kn, axis=-1, keepdims=True) * scale
    m_f = jnp.maximum(m, s_new)
    p_new = jnp.exp(s_new - m_f)
    a = jnp.exp(m - m_f)
    olat = ((a * acc + p_new * kn[:, :LANES]) / (a * l + p_new)).astype(bf16)
    out = _dot(olat[0:1], wuv_ref[0])
    for h in range(1, nh):
        out += _dot(olat[h:h + 1], wuv_ref[h])
    o_ref[0] = out


def _mla_decode(qcat, kcat, wuv, cache_c, cache_r, pt, layer):
    s = qcat.shape[0]
    depth, n_pool, page, _ = cache_c.shape
    npg = pt.shape[1]
    past = npg * page
    ck = min(1024, past)
    grid_spec = pltpu.PrefetchScalarGridSpec(
        num_scalar_prefetch=1, grid=(s,),
        in_specs=[pl.BlockSpec((1, MLA_HEADS, 2 * LANES), lambda i, pt_: (i, 0, 0)),
                  pl.BlockSpec((1, 1, 2 * LANES), lambda i, pt_: (i, 0, 0)),
                  pl.BlockSpec(wuv.shape, lambda i, pt_: (0, 0, 0)),
                  pl.BlockSpec(memory_space=pl.ANY), pl.BlockSpec(memory_space=pl.ANY)],
        out_specs=pl.BlockSpec((1, 1, MLA_HEADS * MLA_V), lambda i, pt_: (i, 0, 0)),
        scratch_shapes=[pltpu.VMEM((2, past, MLA_KV_LORA), f32), pltpu.VMEM((2, past, MLA_ROPE), f32),
                        pltpu.SemaphoreType.DMA((2, 2))])
    out = pl.pallas_call(
        functools.partial(_mla_decode_kernel, base=layer * n_pool, npg=npg, ck=ck),
        grid_spec=grid_spec,
        out_shape=jax.ShapeDtypeStruct((s, 1, MLA_HEADS * MLA_V), f32),
        compiler_params=_cparams(("arbitrary",)),
    )(pt.reshape(-1), qcat.astype(f32).reshape(s, MLA_HEADS, 2 * LANES), kcat.astype(f32).reshape(s, 1, 2 * LANES),
      wuv, cache_c.reshape(depth * n_pool, page, MLA_KV_LORA), cache_r.reshape(depth * n_pool, page, MLA_ROPE))
    return out.reshape(s, MLA_HEADS * MLA_V).astype(bf16)


def _softmax_parts(s, mask):
    sm = s if mask is None else jnp.where(mask, s, NEG_INF)
    m = jnp.max(sm, axis=-1, keepdims=True)
    p = jnp.exp(s - m)
    if mask is not None:
        p = jnp.where(mask, p, 0.0)
    return m, p


def _nsa_decode_kernel(pt_ref, q_ref, misc_ref, snew_ref, wnew_ref, wink_ref, winv_ref,
                       pek_ref, pev_ref, w1k_ref, w1v_ref, w2kl_ref, w2kr_ref, w2vl_ref, w2vr_ref,
                       amat_ref, tsel_ref, twin_ref, tcmp_ref, tnew_ref,
                       ckc_ref, cvc_ref, skc_ref, svc_ref, o_ref,
                       ckbuf, cvbuf, skbuf, svbuf, sems, *, base, npg, ck, k_top):
    slot = _gather_pages(pt_ref, (ckc_ref, cvc_ref, skc_ref, svc_ref), (ckbuf, cvbuf, skbuf, svbuf),
                         tuple(sems.at[n] for n in range(4)), base, npg)
    past = skbuf.shape[1]
    nh = NSA_HEADS
    hd = HEAD_DIM
    scale = hd ** -0.5
    q = q_ref[0]
    qb = q.astype(bf16)
    qpad = jnp.concatenate([q, jnp.zeros_like(q)], axis=1).astype(bf16)
    rowi = _iota((nh, LANES), 0)
    lanei = _iota((nh, LANES), 1)
    gates = 1.0 / (1.0 + jnp.exp(-misc_ref[0]))

    def gate(branch):
        pick = lanei == GATE_LANE + branch * nh + rowi
        return jnp.sum(jnp.where(pick, jnp.broadcast_to(gates, (nh, LANES)), 0.0), axis=-1, keepdims=True)

    kv, vk = _compress_math(ckbuf[slot], cvbuf[slot], pek_ref, pev_ref, w1k_ref, w1v_ref,
                            w2kl_ref, w2kr_ref, w2vl_ref, w2vr_ref)
    nch = kv.shape[0]
    ncol = _iota((nh, nch), 1)
    okc = (CMP_STRIDE * ncol + (CMP_LEN - 1)) <= past
    sc = _dot_nt(qpad, kv.astype(bf16)) * scale + tcmp_ref[...]
    _, p = _softmax_parts(sc, okc)
    l = jnp.sum(p, axis=-1, keepdims=True)
    pc = p * jnp.where(l > 0.0, 1.0 / l, 0.0)
    oc = _dot(pc.astype(bf16), vk.astype(bf16))[:, :hd]
    imp = jnp.broadcast_to(jnp.sum(pc, axis=0, keepdims=True), (nh, nch))

    amat = amat_ref[...]
    nsc = amat.shape[1]
    hi, mid, lo = _split3(imp)
    p_slc = (_dot(hi, amat) + _dot(mid, amat) + _dot(lo, amat))[0:1]
    jl = _iota((1, nsc), 1)
    cur = past // SEL_BLOCK
    forced = (jl == 0) | (jl == cur) | (jl == cur - 1)
    score = jnp.where(jl <= cur, jnp.where(forced, FORCE_SCORE, p_slc), -1.0)
    ii = _iota((nsc, nsc), 0)
    jj = _iota((nsc, nsc), 1)
    srow = jnp.broadcast_to(score, (nsc, nsc))
    scol = jnp.sum(jnp.where(ii == jj, srow, 0.0), axis=-1, keepdims=True)
    beats = jnp.where(scol > srow, 1.0, jnp.where((scol == srow) & (ii < jj), 1.0, 0.0))
    rank = jnp.sum(beats, axis=0, keepdims=True)
    sel = jnp.where((rank < k_top) & (score >= 0.0), 1.0, 0.0)
    sel8 = jnp.broadcast_to(sel, (nh, nsc)).astype(bf16)

    nchunk = past // ck
    bpc = ck // SEL_BLOCK
    jb = _iota((nsc, ck), 0)
    cb = _iota((nsc, ck), 1) // SEL_BLOCK
    far = tnew_ref[:, hd:hd + 1]

    def sel_chunk(t, carry, bias):
        m, l, acc = carry
        off = pl.multiple_of(t * ck, ck)
        k_c = skbuf[slot, pl.ds(off, ck), :].astype(bf16)
        v_c = svbuf[slot, pl.ds(off, ck), :].astype(bf16)
        emat = jnp.where(jb == t * bpc + cb, 1.0, 0.0).astype(bf16)
        mask = _dot(sel8, emat) > 0.5
        s = _dot_nt(qb, k_c) * scale + bias
        m_new = jnp.maximum(m, jnp.max(jnp.where(mask, s, NEG_INF), axis=-1, keepdims=True))
        p = jnp.where(mask, jnp.exp(s - m_new), 0.0)
        a = jnp.exp(m - m_new)
        return m_new, a * l + jnp.sum(p, axis=-1, keepdims=True), a * acc + _dot(p.astype(bf16), v_c)

    init = (jnp.full((nh, 1), NEG_INF, f32), jnp.zeros((nh, 1), f32), jnp.zeros((nh, hd), f32))
    carry = lax.fori_loop(0, nchunk - 1, lambda t, c: sel_chunk(t, c, far), init)
    m, l, acc = sel_chunk(nchunk - 1, carry, tsel_ref[...])

    def add_own(m, l, acc, k_new, v_new, valid):
        s_new = jnp.sum(q * k_new, axis=-1, keepdims=True) * scale + tnew_ref[:, 0:1]
        s_new = jnp.where(valid, s_new, NEG_INF)
        m_f = jnp.maximum(m, s_new)
        p_new = jnp.where(valid, jnp.exp(s_new - m_f), 0.0)
        a = jnp.exp(m - m_f)
        return (a * acc + p_new * v_new) / (a * l + p_new)

    round_b = lambda x: x.astype(bf16).astype(f32)
    snew = round_b(snew_ref[0])
    own_sel = jnp.sum(jnp.where(jl == cur, sel, 0.0), axis=-1, keepdims=True) > 0.5
    osel = add_own(m, l, acc, snew[:, :hd], snew[:, hd:], own_sel)

    kw = wink_ref[...].astype(bf16)
    vw = winv_ref[...].astype(bf16)
    pw = kw.shape[0]
    s = _dot_nt(qb, kw) * scale + twin_ref[...]
    inwin = _iota((nh, pw), 1) > pw - WINDOW
    m, p = _softmax_parts(s, inwin)
    wnew = round_b(wnew_ref[0])
    ow = add_own(m, jnp.sum(p, axis=-1, keepdims=True), _dot(p.astype(bf16), vw), wnew[:, :hd], wnew[:, hd:], True)

    o_ref[0] = gate(0) * oc + gate(1) * osel + gate(2) * ow


def _sel_matrix_decode(nch, n_cmp, ncols):
    a = np.zeros((nch, ncols), np.float32)
    for n in range(n_cmp):
        a[n, n // CH_PER_SEL] += 0.5
        a[n, (n + 1) // CH_PER_SEL] += 0.5
    return a


def _nsa_decode(pf, cw, dtabs, caches, win_k, win_v, pt, layer):
    cache_ck, cache_cv, cache_sk, cache_sv = caches
    s = pf.shape[0]
    depth, n_pool, page, hd = cache_ck.shape
    npg = pt.shape[1]
    past = npg * page
    nch = (past + 1) // CMP_STRIDE
    n_sel = -(-(past + 1) // SEL_BLOCK)
    nsc = -(-n_sel // LANES) * LANES
    ck = min(1024, past)
    pw = win_k.shape[2]
    amat = jnp.asarray(_sel_matrix_decode(nch, nch - 1, nsc), bf16)
    tsel, twin, tcmp, tnew = dtabs
    c = _COL
    q = pf[:, c["nq"]:c["nq"] + NSA_HEADS * hd].reshape(s, NSA_HEADS, hd)
    row3 = lambda name: pf[:, c[name]:c[name] + LANES].reshape(s, 1, LANES)
    seq3 = lambda shape: pl.BlockSpec((1,) + shape, lambda i, pt_: (i, 0, 0))
    fixed = lambda a: pl.BlockSpec(a.shape, lambda i, pt_, n=a.ndim: (0,) * n)
    chunks_per_page = page // CMP_STRIDE
    grid_spec = pltpu.PrefetchScalarGridSpec(
        num_scalar_prefetch=1, grid=(s,),
        in_specs=[seq3((NSA_HEADS, hd)), seq3((1, LANES)), seq3((1, LANES)), seq3((1, LANES)),
                  pl.BlockSpec((None, None, pw, hd), lambda i, pt_: (layer, i, 0, 0)),
                  pl.BlockSpec((None, None, pw, hd), lambda i, pt_: (layer, i, 0, 0))]
                 + [fixed(a) for a in _compress_args(cw)] + [fixed(a) for a in (amat, tsel, twin, tcmp, tnew)]
                 + [pl.BlockSpec(memory_space=pl.ANY)] * 4,
        out_specs=seq3((NSA_HEADS, hd)),
        scratch_shapes=[pltpu.VMEM((2, nch, CMP_STRIDE * hd), f32), pltpu.VMEM((2, nch, CMP_STRIDE * hd), f32),
                        pltpu.VMEM((2, past, hd), f32), pltpu.VMEM((2, past, hd), f32),
                        pltpu.SemaphoreType.DMA((4, 2))])
    chunked = lambda a: a.reshape(depth * n_pool, chunks_per_page, CMP_STRIDE * hd)
    paged = lambda a: a.reshape(depth * n_pool, page, hd)
    out = pl.pallas_call(
        functools.partial(_nsa_decode_kernel, base=layer * n_pool, npg=npg, ck=ck, k_top=min(SEL_TOPK, n_sel)),
        grid_spec=grid_spec,
        out_shape=jax.ShapeDtypeStruct((s, NSA_HEADS, hd), f32),
        compiler_params=_cparams(("arbitrary",)),
    )(pt.reshape(-1), q, row3("misc"), row3("sksv"), row3("wkwv"), win_k, win_v,
      *_compress_args(cw), amat, tsel, twin, tcmp, tnew,
      chunked(cache_ck), chunked(cache_cv), paged(cache_sk), paged(cache_sv))
    return out.reshape(s, NSA_HEADS * hd).astype(bf16)


def _sample_layer(x, lw, tabs, caches, pt, layer):
    cos_t, sin_t, dtabs = tabs
    (c_sbk, c_sbv, c_ckv, c_kr, c_ck, c_cv, c_sk, c_sv, win_k, win_v) = caches
    pf, _ = _proj(x, lw["g_attn_pre"], lw["w_in"])
    qcat, kcat, ckvn, krn = _mla_prep(pf, lw["gq"], lw["gkv"], lw["wq"], lw["wuk"], cos_t, sin_t, 1)
    rows = _cache_rows(pf, ckvn, krn)
    oa = _sb_decode(pf[:, _COL["sbq"]:_COL["sbq"] + 2 * LANES], c_sbk, c_sbv, pt, layer)
    ob = _mla_decode(qcat, kcat, lw["wuv"], c_ckv, c_kr, pt, layer)
    oc = _nsa_decode(pf, lw["cw"], dtabs, (c_ck, c_cv, c_sk, c_sv), win_k, win_v, pt, layer)
    wa, wb, wc = lw["w_out"]
    x = _outproj(oa, ob, oc, wa, wb, wc, lw["g_attn_post"], x)
    x = _ffn(x, lw["g_ffn_pre"], lw["w_gate"], lw["w_up"], lw["w_down"], lw["g_ffn_post"])
    keep = min(WINDOW, win_k.shape[2] + 1)
    rows["wk"] = jnp.concatenate([win_k[layer], rows["wk"][:, None, :]], axis=1)[:, -keep:]
    rows["wv"] = jnp.concatenate([win_v[layer], rows["wv"][:, None, :]], axis=1)[:, -keep:]
    return x, rows


def kernel(x_prompt, x_sample, cache_sb_k, cache_sb_v, cache_mla_ckv, cache_mla_kr, cache_nsa_cmp_k, cache_nsa_cmp_v,
           cache_nsa_sel_k, cache_nsa_sel_v, state_nsa_win_k, state_nsa_win_v, page_table,
           w_in, g_q_lora, w_uq, g_kv_lora, w_uk, w_uv, cmp_pos_k, cmp_w1_k, cmp_w2_k, cmp_pos_v, cmp_w1_v, cmp_w2_v,
           rel_bias, w_out, g_attn_pre, g_attn_post, g_ffn_pre, g_ffn_post, w_gate, w_up, w_down):
    W = dict(w_in=w_in, g_q_lora=g_q_lora, w_uq=w_uq, g_kv_lora=g_kv_lora, w_uk=w_uk, w_uv=w_uv,
             cmp_pos_k=cmp_pos_k, cmp_w1_k=cmp_w1_k, cmp_w2_k=cmp_w2_k,
             cmp_pos_v=cmp_pos_v, cmp_w1_v=cmp_w1_v, cmp_w2_v=cmp_w2_v,
             w_out=w_out, g_attn_pre=g_attn_pre, g_attn_post=g_attn_post,
             g_ffn_pre=g_ffn_pre, g_ffn_post=g_ffn_post, w_gate=w_gate, w_up=w_up, w_down=w_down)
    depth = w_in.shape[0]
    b, t, d = x_prompt.shape
    s = x_sample.shape[0]
    page = cache_sb_k.shape[2]
    past = page_table.shape[1] * page
    lws = [_layer_weights(W, l) for l in range(depth)]

    cos_p, sin_p = _rope_tables(jnp.arange(t, dtype=jnp.int32))
    tsw, tcmp = _bias_tables(rel_bias, t // CMP_STRIDE)
    y = x_prompt.reshape(b * t, d)
    rows_p = []
    for l in range(depth):
        y, rows = _prompt_layer(y, lws[l], b, t, (cos_p, sin_p, tsw, tcmp))
        rows_p.append(rows)
    y_prompt = y.reshape(b, t, d)

    cos_s, sin_s = _rope_tables(jnp.full((s,), past, jnp.int32))
    dtabs = _decode_bias_tables(rel_bias, past, min(1024, past), state_nsa_win_k.shape[2], (past + 1) // CMP_STRIDE)
    caches = (cache_sb_k, cache_sb_v, cache_mla_ckv, cache_mla_kr, cache_nsa_cmp_k, cache_nsa_cmp_v,
              cache_nsa_sel_k, cache_nsa_sel_v, state_nsa_win_k, state_nsa_win_v)
    y = x_sample.reshape(s, d)
    rows_s = []
    for l in range(depth):
        y, rows = _sample_layer(y, lws[l], (cos_s, sin_s, dtabs), caches, page_table, l)
        rows_s.append(rows)
    y_sample = y.reshape(s, 1, d)

    keep_p = min(WINDOW, t)

    def stack(rows_all, name, shape, tail=None):
        parts = [r[name].reshape(shape) for r in rows_all]
        if tail is not None:
            parts = [p[:, -tail:] for p in parts]
        return jnp.stack(parts, axis=0)

    def outputs(rows_all, n, tt, win_tail):
        kvh = (n, tt, SB_KV_HEADS, HEAD_DIM)
        return (stack(rows_all, "sb_k", kvh), stack(rows_all, "sb_v", kvh),
                stack(rows_all, "ckv", (n, tt, MLA_KV_LORA)), stack(rows_all, "kr", (n, tt, MLA_ROPE)),
                stack(rows_all, "ck", (n, tt, HEAD_DIM)), stack(rows_all, "cv", (n, tt, HEAD_DIM)),
                stack(rows_all, "sk", (n, tt, HEAD_DIM)), stack(rows_all, "sv", (n, tt, HEAD_DIM)),
                stack(rows_all, "wk", (n, -1, HEAD_DIM), win_tail), stack(rows_all, "wv", (n, -1, HEAD_DIM), win_tail))

    return (y_prompt, y_sample) + outputs(rows_p, b, t, keep_p) + outputs(rows_s, s, 1, None)
```

```python
import functools
import math

import numpy as np
import jax
import jax.numpy as jnp
from jax import lax
from jax.experimental import pallas as pl
from jax.experimental.pallas import tpu as pltpu

f32 = jnp.float32
bf16 = jnp.bfloat16

HEAD_DIM = 64
SB_HEADS = 4
SB_KV_HEADS = 2
MLA_HEADS = 4
MLA_Q_LORA = 256
MLA_KV_LORA = 128
MLA_NOPE = 64
MLA_ROPE = 32
MLA_V = 64
ROPE_BASE = 10000.0
NSA_HEADS = 8
CMP_STRIDE = 16
CMP_LEN = 2 * CMP_STRIDE
CMP_HID = 256
SEL_BLOCK = 64
CH_PER_SEL = SEL_BLOCK // CMP_STRIDE
SEL_TOPK = 16
WINDOW = 512
N_BUCKETS = 32
MAX_DISTANCE = 128
RMS_EPS = 1e-6
NEG_INF = -1e30
FORCE_SCORE = 1e4

LOG2E = math.log2(math.e)
LANES = 128
VMEM_LIMIT = 56 * 1024 * 1024

_COL = dict(nq=0, sbq=512, cq=768, sbk=1024, sbv=1152, ckv=1280, ckcv=1408, sksv=1536, svsk=1664,
            wkwv=1792, wvwk=1920, misc=2048, rot=2176)
NP = 2304
GATE_LANE = MLA_ROPE


def _cparams(sem):
    return pltpu.CompilerParams(dimension_semantics=sem, vmem_limit_bytes=VMEM_LIMIT)


def _rms(x, g):
    return x * lax.rsqrt(jnp.mean(x * x, axis=-1, keepdims=True) + RMS_EPS) * g


def _dot(a, b):
    return jnp.dot(a, b, preferred_element_type=f32)


def _dot_nt(a, b):
    return lax.dot_general(a, b, (((1,), (1,)), ((), ())), preferred_element_type=f32)


def _iota(shape, dim):
    return lax.broadcasted_iota(jnp.int32, shape, dim)


def _t5_thresholds():
    exact = N_BUCKETS // 2
    d = np.arange(0, 2 * MAX_DISTANCE + 1)
    ratio = np.log(np.maximum(d, 1).astype(np.float32) / exact) / math.log(MAX_DISTANCE / exact)
    large = np.minimum(exact + (ratio * (N_BUCKETS - exact)).astype(np.int32), N_BUCKETS - 1)
    bucket = np.where(d < exact, d, large)
    assert np.all(np.diff(bucket) >= 0)
    return [int(np.argmax(bucket >= b)) for b in range(1, N_BUCKETS)]


_T5_THR = _t5_thresholds()


def _bias_lookup(dist, rb_ref, h):
    val = jnp.full(dist.shape, rb_ref[h], f32)
    for b in range(1, N_BUCKETS):
        val = jnp.where(dist >= _T5_THR[b - 1], rb_ref[b * NSA_HEADS + h], val)
    return val


def _bias_tables_kernel(rb_ref, tsw_ref, tcmp_ref, *, wc):
    r = _iota((LANES, LANES), 0)
    c = _iota((LANES, LANES), 1)
    rc = _iota((LANES, wc), 0)
    cc = _iota((LANES, wc), 1)
    dcmp = jnp.maximum(rc - CMP_STRIDE * cc + (LANES - CMP_STRIDE + 1), 0)
    for h in range(NSA_HEADS):
        far = rb_ref[(N_BUCKETS - 1) * NSA_HEADS + h]
        d0 = r - c
        tsw_ref[h, 0] = jnp.where(d0 >= 0, _bias_lookup(jnp.maximum(d0, 0), rb_ref, h) * LOG2E, NEG_INF)
        tsw_ref[h, 1] = _bias_lookup(d0 + LANES, rb_ref, h) * LOG2E
        tsw_ref[h, 2] = jnp.full((LANES, LANES), far * LOG2E, f32)
        tcmp_ref[h] = jnp.where(cc < 2 * LANES // CMP_STRIDE, _bias_lookup(dcmp, rb_ref, h), far)


def _bias_tables(rel_bias, wc):
    return pl.pallas_call(
        functools.partial(_bias_tables_kernel, wc=wc),
        out_shape=(jax.ShapeDtypeStruct((NSA_HEADS, 3, LANES, LANES), f32),
                   jax.ShapeDtypeStruct((NSA_HEADS, LANES, wc), f32)),
        in_specs=[pl.BlockSpec(memory_space=pltpu.SMEM)],
    )(rel_bias.reshape(-1))


def _proj_kernel(x_ref, g_ref, w_ref, of_ref, ob_ref):
    h = _rms(x_ref[...], g_ref[...]).astype(bf16)
    y = _dot(h, w_ref[...])
    of_ref[...] = y
    ob_ref[...] = y.astype(bf16)


def _proj(x, g, w):
    m, d = x.shape
    tm = min(256, m)
    return pl.pallas_call(
        _proj_kernel,
        grid=(m // tm,),
        in_specs=[pl.BlockSpec((tm, d), lambda i: (i, 0)),
                  pl.BlockSpec((1, d), lambda i: (0, 0)),
                  pl.BlockSpec((d, NP), lambda i: (0, 0))],
        out_specs=(pl.BlockSpec((tm, NP), lambda i: (i, 0)),
                   pl.BlockSpec((tm, NP), lambda i: (i, 0))),
        out_shape=(jax.ShapeDtypeStruct((m, NP), f32), jax.ShapeDtypeStruct((m, NP), bf16)),
        compiler_params=_cparams(("parallel",)),
    )(x, g.reshape(1, d), w)


def _outproj_kernel(oa_ref, ob_ref, oc_ref, wa_ref, wb_ref, wc_ref, g_ref, x_ref, o_ref):
    y = _dot(oa_ref[...], wa_ref[...]) + _dot(ob_ref[...], wb_ref[...]) + _dot(oc_ref[...], wc_ref[...])
    o_ref[...] = x_ref[...] + _rms(y, g_ref[...])


def _outproj(oa, ob, oc, wa, wb, wc, g, x):
    m, d = x.shape
    tm = min(512, m)
    row = lambda i: (i, 0)
    fixed = lambda i: (0, 0)
    return pl.pallas_call(
        _outproj_kernel,
        grid=(m // tm,),
        in_specs=[pl.BlockSpec((tm, oa.shape[1]), row), pl.BlockSpec((tm, ob.shape[1]), row),
                  pl.BlockSpec((tm, oc.shape[1]), row),
                  pl.BlockSpec(wa.shape, fixed), pl.BlockSpec(wb.shape, fixed), pl.BlockSpec(wc.shape, fixed),
                  pl.BlockSpec((1, d), fixed), pl.BlockSpec((tm, d), row)],
        out_specs=pl.BlockSpec((tm, d), row),
        out_shape=jax.ShapeDtypeStruct((m, d), f32),
        compiler_params=_cparams(("parallel",)),
    )(oa, ob, oc, wa, wb, wc, g.reshape(1, d), x)


def _ffn_kernel(x_ref, gpre_ref, wg_ref, wu_ref, wd_ref, gpost_ref, o_ref, hf_ref, acc_ref):
    j = pl.program_id(1)

    @pl.when(j == 0)
    def _():
        hf_ref[...] = _rms(x_ref[...], gpre_ref[...]).astype(bf16)
        acc_ref[...] = jnp.zeros_like(acc_ref)

    hf = hf_ref[...]
    a = _dot(hf, wg_ref[...])
    u = _dot(hf, wu_ref[...])
    act = (a / (1.0 + jnp.exp(-a))) * u
    acc_ref[...] += _dot(act.astype(bf16), wd_ref[...])

    @pl.when(j == pl.num_programs(1) - 1)
    def _():
        o_ref[...] = x_ref[...] + _rms(acc_ref[...], gpost_ref[...])


def _ffn(x, gpre, wg, wu, wd, gpost):
    m, d = x.shape
    ff = wg.shape[1]
    tm = min(512, m)
    tf = ff // 2 if (ff // 2) % LANES == 0 else ff
    return pl.pallas_call(
        _ffn_kernel,
        grid=(m // tm, ff // tf),
        in_specs=[pl.BlockSpec((tm, d), lambda i, j: (i, 0)),
                  pl.BlockSpec((1, d), lambda i, j: (0, 0)),
                  pl.BlockSpec((d, tf), lambda i, j: (0, j)),
                  pl.BlockSpec((d, tf), lambda i, j: (0, j)),
                  pl.BlockSpec((tf, d), lambda i, j: (j, 0)),
                  pl.BlockSpec((1, d), lambda i, j: (0, 0))],
        out_specs=pl.BlockSpec((tm, d), lambda i, j: (i, 0)),
        out_shape=jax.ShapeDtypeStruct((m, d), f32),
        scratch_shapes=[pltpu.VMEM((tm, d), bf16), pltpu.VMEM((tm, d), f32)],
        compiler_params=_cparams(("parallel", "arbitrary")),
    )(x, gpre.reshape(1, d), wg, wu, wd, gpost.reshape(1, d))


def _mla_prep_kernel(cq_ref, ckv_ref, misc_ref, rot_ref, gq_ref, gkv_ref, wq_ref, wuk_ref, cos_ref, sin_ref,
                     qcat_ref, kcat_ref, ckvn_ref, krn_ref):
    cos = cos_ref[...]
    sin = sin_ref[...]
    cq = _rms(cq_ref[...], gq_ref[...]).astype(bf16)
    qa = _dot(cq, wq_ref[...])
    nh = MLA_HEADS
    for h in range(nh):
        nope = qa[:, h * LANES:(h + 1) * LANES].astype(bf16)
        qlat = _dot(nope, wuk_ref[h])
        rope = qa[:, (nh + h) * LANES:(nh + h + 1) * LANES] * cos + qa[:, (2 * nh + h) * LANES:(2 * nh + h + 1) * LANES] * sin
        qcat_ref[:, 2 * h * LANES:(2 * h + 1) * LANES] = qlat.astype(bf16)
        qcat_ref[:, (2 * h + 1) * LANES:(2 * h + 2) * LANES] = rope.astype(bf16)
    ckv = _rms(ckv_ref[...], gkv_ref[...])
    kr = misc_ref[...] * cos + rot_ref[...] * sin
    ckvn_ref[...] = ckv
    krn_ref[...] = kr
    kcat_ref[:, :LANES] = ckv.astype(bf16)
    lane = _iota(kr.shape, 1)
    kcat_ref[:, LANES:] = jnp.where(lane < MLA_ROPE, kr, 1.0).astype(bf16)


def _mla_prep(pf, gq, gkv, wq, wuk, cos_t, sin_t, n_pos_blocks):
    m = pf.shape[0]
    tm = min(256, m)
    cb = lambda name, w: _COL[name] // w
    tab = lambda i: (i % n_pos_blocks, 0)
    fixed2 = lambda i: (0, 0)
    return pl.pallas_call(
        _mla_prep_kernel,
        grid=(m // tm,),
        in_specs=[pl.BlockSpec((tm, MLA_Q_LORA), lambda i: (i, cb("cq", MLA_Q_LORA))),
                  pl.BlockSpec((tm, LANES), lambda i: (i, cb("ckv", LANES))),
                  pl.BlockSpec((tm, LANES), lambda i: (i, cb("misc", LANES))),
                  pl.BlockSpec((tm, LANES), lambda i: (i, cb("rot", LANES))),
                  pl.BlockSpec((1, MLA_Q_LORA), fixed2), pl.BlockSpec((1, MLA_KV_LORA), fixed2),
                  pl.BlockSpec(wq.shape, fixed2), pl.BlockSpec(wuk.shape, lambda i: (0, 0, 0)),
                  pl.BlockSpec((tm, LANES), tab), pl.BlockSpec((tm, LANES), tab)],
        out_specs=(pl.BlockSpec((tm, 2 * LANES * MLA_HEADS), lambda i: (i, 0)),
                   pl.BlockSpec((tm, 2 * LANES), lambda i: (i, 0)),
                   pl.BlockSpec((tm, LANES), lambda i: (i, 0)),
                   pl.BlockSpec((tm, LANES), lambda i: (i, 0))),
        out_shape=(jax.ShapeDtypeStruct((m, 2 * LANES * MLA_HEADS), bf16),
                   jax.ShapeDtypeStruct((m, 2 * LANES), bf16),
                   jax.ShapeDtypeStruct((m, LANES), f32),
                   jax.ShapeDtypeStruct((m, LANES), f32)),
        compiler_params=_cparams(("parallel",)),
    )(pf, pf, pf, pf, gq.reshape(1, -1), gkv.reshape(1, -1), wq, wuk, cos_t, sin_t)


def _compress_math(chk, chv, pek_ref, pev_ref, w1k_ref, w1v_ref, w2kl_ref, w2kr_ref, w2vl_ref, w2vr_ref):
    nch = chk.shape[0]

    def hidden(ch, pe_ref, w1_ref):
        first = _dot((ch + pe_ref[0:1, :]).astype(bf16), w1_ref[0])
        second = _dot((ch + pe_ref[1:2, :]).astype(bf16), w1_ref[1])
        pre = first + pltpu.roll(second, nch - 1, axis=0)
        return (pre / (1.0 + jnp.exp(-pre))).astype(bf16)

    hk = hidden(chk, pek_ref, w1k_ref)
    hv = hidden(chv, pev_ref, w1v_ref)
    kv = _dot(hk, w2kl_ref[...]) + _dot(hv, w2vr_ref[...])
    vk = _dot(hv, w2vl_ref[...]) + _dot(hk, w2kr_ref[...])
    return kv, vk


def _compress_kernel(chk_ref, chv_ref, pek_ref, pev_ref, w1k_ref, w1v_ref, w2kl_ref, w2kr_ref, w2vl_ref, w2vr_ref,
                     kv_ref, vk_ref):
    kv, vk = _compress_math(chk_ref[0], chv_ref[0], pek_ref, pev_ref, w1k_ref, w1v_ref,
                            w2kl_ref, w2kr_ref, w2vl_ref, w2vr_ref)
    kv_ref[0] = kv.astype(bf16)
    vk_ref[0] = vk.astype(bf16)


def _compress_specs(cw):
    c2 = lambda *_: (0, 0)
    c3 = lambda *_: (0, 0, 0)
    return [pl.BlockSpec(cw["pek"].shape, c2), pl.BlockSpec(cw["pev"].shape, c2),
            pl.BlockSpec(cw["w1k"].shape, c3), pl.BlockSpec(cw["w1v"].shape, c3),
            pl.BlockSpec(cw["w2kl"].shape, c2), pl.BlockSpec(cw["w2kr"].shape, c2),
            pl.BlockSpec(cw["w2vl"].shape, c2), pl.BlockSpec(cw["w2vr"].shape, c2)]


def _compress_args(cw):
    return (cw["pek"], cw["pev"], cw["w1k"], cw["w1v"], cw["w2kl"], cw["w2kr"], cw["w2vl"], cw["w2vr"])


def _compress(chk, chv, cw):
    b, nch, cl = chk.shape
    blk = pl.BlockSpec((1, nch, cl), lambda i: (i, 0, 0))
    out = pl.BlockSpec((1, nch, LANES), lambda i: (i, 0, 0))
    return pl.pallas_call(
        _compress_kernel,
        grid=(b,),
        in_specs=[blk, blk] + _compress_specs(cw),
        out_specs=(out, out),
        out_shape=(jax.ShapeDtypeStruct((b, nch, LANES), bf16),) * 2,
        compiler_params=_cparams(("parallel",)),
    )(chk, chv, *_compress_args(cw))


def _softplus(z):
    return jnp.maximum(z, 0.0) + jnp.log(1.0 + jnp.exp(-jnp.abs(z)))


def _sb_tile(qm, k_t, v_t, valid, carry, tri, ones):
    z = _dot_nt(qm, k_t) * (HEAD_DIM ** -0.5)
    lr = jnp.where(valid, -_softplus(z), 0.0)
    hi = lr.astype(bf16)
    lo = (lr - hi.astype(f32)).astype(bf16)
    after_in = _dot(hi, tri) + _dot(lo, tri)
    tot = _dot(hi, ones) + _dot(lo, ones)
    reps = z.shape[1] // LANES
    after = after_in + (jnp.concatenate([carry] * reps, axis=1) if reps > 1 else carry)
    w = jnp.where(valid, jnp.exp(z + lr + after), 0.0)
    return _dot(w.astype(bf16), v_t), tot


def _sb_prompt_kernel(q_ref, k_ref, v_ref, o_ref, acc_ref, car_ref, *, tq, tk):
    i = pl.program_id(1)
    lane = _iota((tq, LANES), 1)
    rk = _iota((tk, tk), 0)
    ck = _iota((tk, tk), 1)
    tri = jnp.where(rk > ck, 1.0, 0.0).astype(bf16)
    ones = jnp.ones((tk, LANES), bf16)
    qpos = i * tq + _iota((tq, tk), 0)
    kcol = _iota((tq, tk), 1)
    acc_ref[...] = jnp.zeros_like(acc_ref)
    car_ref[...] = jnp.zeros_like(car_ref)
    heads = [(g, h) for g in range(2) for h in range(SB_KV_HEADS)]
    qms = []
    for g, h in heads:
        qp = q_ref[:, g * LANES:(g + 1) * LANES]
        qms.append(jnp.where((lane // HEAD_DIM) == h, qp, jnp.zeros_like(qp)))
    nkb = (i * tq + tq - 1) // tk + 1

    def body(t, c):
        kb = nkb - 1 - t
        k_t = k_ref[pl.ds(pl.multiple_of(kb * tk, tk), tk), :]
        v_t = v_ref[pl.ds(pl.multiple_of(kb * tk, tk), tk), :]
        valid = (kb * tk + kcol) < qpos
        for n in range(len(heads)):
            pv, tot = _sb_tile(qms[n], k_t, v_t, valid, car_ref[n], tri, ones)
            acc_ref[n] += pv
            car_ref[n] += tot
        return c

    lax.fori_loop(0, nkb, body, 0)
    for g in range(2):
        o_ref[:, g * LANES:(g + 1) * LANES] = jnp.where(lane < HEAD_DIM, acc_ref[2 * g], acc_ref[2 * g + 1]).astype(bf16)


def _sb_prompt(pb, b, t):
    tq = min(256, t)
    tk = tq
    nq = t // tq
    return pl.pallas_call(
        functools.partial(_sb_prompt_kernel, tq=tq, tk=tk),
        grid=(b, nq),
        in_specs=[pl.BlockSpec((tq, 2 * LANES), lambda bb, i: (bb * nq + i, _COL["sbq"] // (2 * LANES))),
                  pl.BlockSpec((t, LANES), lambda bb, i: (bb, _COL["sbk"] // LANES)),
                  pl.BlockSpec((t, LANES), lambda bb, i: (bb, _COL["sbv"] // LANES))],
        out_specs=pl.BlockSpec((tq, 2 * LANES), lambda bb, i: (bb * nq + i, 0)),
        out_shape=jax.ShapeDtypeStruct((b * t, 2 * LANES), bf16),
        scratch_shapes=[pltpu.VMEM((SB_HEADS, tq, LANES), f32), pltpu.VMEM((SB_HEADS, tq, LANES), f32)],
        compiler_params=_cparams(("parallel", "parallel")),
    )(pb, pb, pb)


def _mla_prompt_kernel(q_ref, k_ref, wuv_ref, o_ref, m_ref, acc_ref, *, tq, tk):
    i = pl.program_id(1)
    nh = MLA_HEADS
    qs = jnp.concatenate([q_ref[:, h * 2 * LANES:(h + 1) * 2 * LANES] for h in range(nh)], axis=0)
    scale = (MLA_NOPE + MLA_ROPE) ** -0.5
    qpos = i * tq + (_iota((nh * tq, tk), 0) % tq)
    kcol = _iota((nh * tq, tk), 1)
    m_ref[...] = jnp.full(m_ref.shape, NEG_INF, f32)
    acc_ref[...] = jnp.zeros_like(acc_ref)
    nkb = (i * tq + tq - 1) // tk + 1

    def body(kb, c):
        k_t = k_ref[pl.ds(pl.multiple_of(kb * tk, tk), tk), :]
        s = _dot_nt(qs, k_t) * scale
        s = jnp.where((kb * tk + kcol) <= qpos, s, NEG_INF)
        m_old = m_ref[...]
        m_new = jnp.maximum(m_old, jnp.max(s, axis=-1, keepdims=True))
        p = jnp.exp(s - m_new)
        acc_ref[...] = jnp.exp(m_old - m_new) * acc_ref[...] + _dot(p.astype(bf16), k_t)
        m_ref[...] = m_new
        return c

    lax.fori_loop(0, nkb, body, 0)
    acc = acc_ref[...]
    olat = (acc[:, :LANES] / acc[:, 2 * LANES - 1:2 * LANES]).astype(bf16)
    out = _dot(olat[0:tq], wuv_ref[0])
    for h in range(1, nh):
        out += _dot(olat[h * tq:(h + 1) * tq], wuv_ref[h])
    o_ref[...] = out.astype(bf16)


def _mla_prompt(qcat, kcat, wuv, b, t):
    tq = min(128, t)
    tk = min(512, t)
    nq = t // tq
    return pl.pallas_call(
        functools.partial(_mla_prompt_kernel, tq=tq, tk=tk),
        grid=(b, nq),
        in_specs=[pl.BlockSpec((tq, qcat.shape[1]), lambda bb, i: (bb * nq + i, 0)),
                  pl.BlockSpec((t, 2 * LANES), lambda bb, i: (bb, 0)),
                  pl.BlockSpec(wuv.shape, lambda bb, i: (0, 0, 0))],
        out_specs=pl.BlockSpec((tq, MLA_HEADS * MLA_V), lambda bb, i: (bb * nq + i, 0)),
        out_shape=jax.ShapeDtypeStruct((b * t, MLA_HEADS * MLA_V), bf16),
        scratch_shapes=[pltpu.VMEM((MLA_HEADS * tq, 1), f32), pltpu.VMEM((MLA_HEADS * tq, 2 * LANES), f32)],
        compiler_params=_cparams(("parallel", "parallel")),
    )(qcat, kcat, wuv)


def _split3(x):
    hi = x.astype(bf16)
    r1 = x - hi.astype(f32)
    mid = r1.astype(bf16)
    lo = (r1 - mid.astype(f32)).astype(bf16)
    return hi, mid, lo


def _wide_update(s, mask, v_t, m_ref, acc_ref, idx):
    reps = s.shape[1] // LANES
    m_old = m_ref[idx]
    m_new = jnp.maximum(m_old, jnp.max(jnp.where(mask, s, NEG_INF), axis=-1, keepdims=True))
    m_wide = jnp.concatenate([m_new] * reps, axis=1) if reps > 1 else m_new
    p = jnp.where(mask, jnp.exp2(s - m_wide), 0.0)
    acc_ref[idx] = jnp.exp2(m_old - m_new) * acc_ref[idx] + _dot(p.astype(bf16), v_t)
    m_ref[idx] = m_new


def _pair_outputs(acc_ref, gates, branch, lane_lo):
    outs = []
    for p in range(NSA_HEADS // 2):
        pair = []
        for h in (2 * p, 2 * p + 1):
            a = acc_ref[h]
            o = a * pltpu.roll(1.0 / a, HEAD_DIM, axis=1)
            gl = GATE_LANE + branch * NSA_HEADS + h
            pair.append(o * gates[:, gl:gl + 1])
        outs.append(jnp.where(lane_lo, pair[0], pair[1]))
    return outs


def _nsa_prompt_kernel(rb_ref, q_ref, misc_ref, kvs_ref, vks_ref, kvw_ref, vkw_ref, kvc_ref, vkc_ref,
                       tsw_ref, tcmp_ref, amat_ref, o_ref, m_ref, acc_ref, *, tq, n_sel, k_top, tf, nw):
    i = pl.program_id(1)
    nh = NSA_HEADS
    npair = nh // 2
    wc = kvc_ref.shape[1]
    scale = HEAD_DIM ** -0.5
    lane = _iota((tq, LANES), 1)
    lane_lo = lane < HEAD_DIM
    row = _iota((tq, LANES), 0)
    qpos = i * tq + row

    qe, qo = [], []
    for p in range(npair):
        qp = q_ref[:, p * LANES:(p + 1) * LANES]
        qe.append(jnp.where(lane_lo, qp, jnp.zeros_like(qp)))
        qo.append(jnp.where(lane_lo, jnp.zeros_like(qp), qp))
    qe = jnp.concatenate(qe, axis=0)
    qo = jnp.concatenate(qo, axis=0)

    def head_rows(se, so, h):
        src = se if h % 2 == 0 else so
        return src[(h // 2) * tq:(h // 2 + 1) * tq]

    gates = 1.0 / (1.0 + jnp.exp(-misc_ref[...]))

    kvc = kvc_ref[0]
    vkc = vkc_ref[0]
    se = _dot_nt(qe, kvc) * scale
    so = _dot_nt(qo, vkc) * scale
    ncol = _iota((tq, wc), 1)
    qpos_c = i * tq + _iota((tq, wc), 0)
    okc = (CMP_STRIDE * ncol + (CMP_LEN - 1)) <= qpos_c
    shift = (i * (tq // CMP_STRIDE) + wc - (LANES // CMP_STRIDE + 1)) % wc
    imp = jnp.zeros((tq, wc), f32)
    oc = []
    for h in range(nh):
        s = head_rows(se, so, h) + pltpu.roll(tcmp_ref[h], shift, axis=1)
        sm = jnp.where(okc, s, NEG_INF)
        mx = jnp.max(sm, axis=-1, keepdims=True)
        p = jnp.where(okc, jnp.exp(s - mx), 0.0)
        l = jnp.sum(p, axis=-1, keepdims=True)
        pc = p * jnp.where(l > 0.0, 1.0 / l, 0.0)
        imp = imp + pc
        oc.append(_dot(pc.astype(bf16), vkc if h % 2 == 0 else kvc))
    out = []
    for p in range(npair):
        ge = gates[:, GATE_LANE + 2 * p:GATE_LANE + 2 * p + 1]
        go = gates[:, GATE_LANE + 2 * p + 1:GATE_LANE + 2 * p + 2]
        out.append(jnp.where(lane_lo, oc[2 * p] * ge, oc[2 * p + 1] * go))

    amat = amat_ref[...]
    hi, mid, lo = _split3(imp)
    p_slc = _dot(hi, amat) + _dot(mid, amat) + _dot(lo, amat)
    cur = qpos // SEL_BLOCK
    forced = (lane == 0) | (lane == cur) | (lane == cur - 1)
    score = jnp.where(lane <= cur, jnp.where(forced, FORCE_SCORE, p_slc), -1.0)
    st = score.T[0:n_sel]
    jrow = _iota((n_sel, tq), 0)
    rank = jnp.zeros((n_sel, tq), f32)
    for b in range(n_sel):
        rb = st[b:b + 1, :]
        rank = rank + jnp.where(rb > st, 1.0, jnp.where((rb == st) & (jrow > b), 1.0, 0.0))
    sel_t = jnp.where((rank < k_top) & (st >= 0.0), 1.0, 0.0)
    if n_sel < LANES:
        sel_t = jnp.concatenate([sel_t, jnp.zeros((LANES - n_sel, tq), f32)], axis=0)
    sel = sel_t.T.astype(bf16)

    def reset():
        m_ref[...] = jnp.full(m_ref.shape, NEG_INF, f32)
        acc_ref[...] = jnp.zeros_like(acc_ref)

    sc2 = scale * LOG2E
    one_b = jnp.ones((), bf16)
    kb0 = jnp.maximum(i - WINDOW // LANES, 0)
    near0 = pl.multiple_of(kb0 * LANES, LANES)
    sub = nw // LANES

    def attend(kv_ref, vk_ref, off, width, bias_fn, mask):
        kv = kv_ref[pl.ds(off, width), :]
        vk = vk_ref[pl.ds(off, width), :]
        se = _dot_nt(qe, kv) * sc2
        so = _dot_nt(qo, vk) * sc2
        lk = _iota(kv.shape, 1) < HEAD_DIM
        v_even = jnp.where(lk, vk, one_b)
        v_odd = jnp.where(lk, one_b, kv)
        for h in range(nh):
            s = bias_fn(head_rows(se, so, h), h)
            _wide_update(s, mask, v_even if h % 2 == 0 else v_odd, m_ref, acc_ref, h)

    def near_bias(s, h):
        tiles = [tsw_ref[h, jnp.clip(i - kb0 - j, 0, 2)] for j in range(sub)]
        return s + (jnp.concatenate(tiles, axis=1) if sub > 1 else tiles[0])

    qpos_n = i * tq + _iota((tq, nw), 0)
    kpos_n = near0 + _iota((tq, nw), 1)
    causal_n = kpos_n <= qpos_n

    reset()
    jf = _iota((LANES, tf), 0)
    cf = _iota((LANES, tf), 1)
    kcol_f = _iota((tq, tf), 1)

    def far_body(t, c):
        off = pl.multiple_of(t * tf, tf)
        emat = jnp.where(jf == (tf // SEL_BLOCK) * t + cf // SEL_BLOCK, 1.0, 0.0).astype(bf16)
        mask = (_dot(sel, emat) > 0.5) & (off + kcol_f < near0)
        attend(kvs_ref, vks_ref, off, tf, lambda s, h: s + rb_ref[(N_BUCKETS - 1) * nh + h] * LOG2E, mask)
        return c

    lax.fori_loop(0, (near0 + tf - 1) // tf, far_body, 0)
    jn = _iota((LANES, nw), 0)
    cn = _iota((LANES, nw), 1)
    emat = jnp.where(jn == (LANES // SEL_BLOCK) * kb0 + cn // SEL_BLOCK, 1.0, 0.0).astype(bf16)
    attend(kvs_ref, vks_ref, near0, nw, near_bias, (_dot(sel, emat) > 0.5) & causal_n)
    for p, blk in enumerate(_pair_outputs(acc_ref, gates, 1, lane_lo)):
        out[p] = out[p] + blk

    reset()
    attend(kvw_ref, vkw_ref, near0, nw, near_bias, causal_n & (kpos_n > qpos_n - WINDOW))
    for p, blk in enumerate(_pair_outputs(acc_ref, gates, 2, lane_lo)):
        out[p] = out[p] + blk
    for p in range(npair):
        o_ref[:, p * LANES:(p + 1) * LANES] = out[p].astype(bf16)


def _sel_matrix(wc, n_cmp):
    a = np.zeros((wc, LANES), np.float32)
    for n in range(n_cmp):
        a[n, n // CH_PER_SEL] += 0.5
        a[n, (n + 1) // CH_PER_SEL] += 0.5
    return a


def _nsa_prompt(pb, pf, kvc, vkc, rel_bias, tsw, tcmp, b, t):
    tq = LANES
    nq = t // tq
    wc = kvc.shape[1]
    n_sel = t // SEL_BLOCK
    amat = jnp.asarray(_sel_matrix(wc, wc - 1), bf16)
    full = lambda name: pl.BlockSpec((t, LANES), lambda bb, i, n=name: (bb, _COL[n] // LANES))
    tf = min(4 * LANES, t)
    nw = min(WINDOW + LANES, t)
    return pl.pallas_call(
        functools.partial(_nsa_prompt_kernel, tq=tq, n_sel=n_sel, k_top=min(SEL_TOPK, n_sel), tf=tf, nw=nw),
        grid=(b, nq),
        in_specs=[pl.BlockSpec(memory_space=pltpu.SMEM),
                  pl.BlockSpec((tq, NSA_HEADS * HEAD_DIM), lambda bb, i: (bb * nq + i, 0)),
                  pl.BlockSpec((tq, LANES), lambda bb, i: (bb * nq + i, _COL["misc"] // LANES)),
                  full("sksv"), full("svsk"), full("wkwv"), full("wvwk"),
                  pl.BlockSpec((1, wc, LANES), lambda bb, i: (bb, 0, 0)),
                  pl.BlockSpec((1, wc, LANES), lambda bb, i: (bb, 0, 0)),
                  pl.BlockSpec(tsw.shape, lambda bb, i: (0, 0, 0, 0)),
                  pl.BlockSpec(tcmp.shape, lambda bb, i: (0, 0, 0)),
                  pl.BlockSpec(amat.shape, lambda bb, i: (0, 0))],
        out_specs=pl.BlockSpec((tq, NSA_HEADS * HEAD_DIM), lambda bb, i: (bb * nq + i, 0)),
        out_shape=jax.ShapeDtypeStruct((b * t, NSA_HEADS * HEAD_DIM), bf16),
        scratch_shapes=[pltpu.VMEM((NSA_HEADS, tq, LANES), f32), pltpu.VMEM((NSA_HEADS, tq, LANES), f32)],
        compiler_params=_cparams(("parallel", "parallel")),
    )(rel_bias.reshape(-1), pb, pf, pb, pb, pb, pb, kvc, vkc, tsw, tcmp, amat)


_IN_SIZES = (SB_HEADS * HEAD_DIM, SB_KV_HEADS * HEAD_DIM, SB_KV_HEADS * HEAD_DIM,
             MLA_Q_LORA, MLA_KV_LORA, MLA_ROPE,
             NSA_HEADS * HEAD_DIM, HEAD_DIM, HEAD_DIM, HEAD_DIM, HEAD_DIM, HEAD_DIM, HEAD_DIM,
             3 * NSA_HEADS)
_SB_HEAD_ORDER = (0, 2, 1, 3)


def _rot_half_cols(w):
    half = w.shape[-1] // 2
    return jnp.concatenate([-w[..., half:], w[..., :half]], axis=-1)


def _prep_w_in(w):
    offs = np.concatenate([[0], np.cumsum(_IN_SIZES)])
    (sbq, sbk, sbv, cq, ckv, kr, nq, ck, cv, sk, sv, wk, wv, gl) = [w[:, offs[n]:offs[n + 1]] for n in range(len(_IN_SIZES))]
    sbq = jnp.concatenate([sbq[:, hh * HEAD_DIM:(hh + 1) * HEAD_DIM] for hh in _SB_HEAD_ORDER], axis=1)
    z = lambda n: jnp.zeros((w.shape[0], n), w.dtype)
    cols = [nq, sbq, cq, sbk, sbv, ckv, ck, cv, sk, sv, sv, sk, wk, wv, wv, wk,
            kr, gl, z(LANES - MLA_ROPE - 3 * NSA_HEADS), _rot_half_cols(kr), z(LANES - MLA_ROPE)]
    out = jnp.concatenate(cols, axis=1)
    assert out.shape[1] == NP
    return out.astype(bf16)


def _prep_w_uq(w_uq):
    d = w_uq.shape[0]
    per = w_uq.reshape(d, MLA_HEADS, MLA_NOPE + MLA_ROPE)
    nope = jnp.pad(per[:, :, :MLA_NOPE], ((0, 0), (0, 0), (0, LANES - MLA_NOPE)))
    rope = per[:, :, MLA_NOPE:]
    rpad = ((0, 0), (0, 0), (0, LANES - MLA_ROPE))
    blocks = [nope, jnp.pad(rope, rpad), jnp.pad(_rot_half_cols(rope), rpad)]
    return jnp.concatenate([x.reshape(d, MLA_HEADS * LANES) for x in blocks], axis=1).astype(bf16)


def _prep_w_uk(w_uk):
    t = w_uk.reshape(MLA_KV_LORA, MLA_HEADS, MLA_NOPE).transpose(1, 2, 0)
    return jnp.pad(t, ((0, 0), (0, LANES - MLA_NOPE), (0, 0))).astype(bf16)


def _prep_w_uv(w_uv):
    per = w_uv.reshape(MLA_KV_LORA, MLA_HEADS, MLA_V)
    out = jnp.zeros((MLA_HEADS, MLA_KV_LORA, MLA_HEADS * MLA_V), w_uv.dtype)
    for h in range(MLA_HEADS):
        out = out.at[h, :, h * MLA_V:(h + 1) * MLA_V].set(per[:, h, :])
    return out.astype(bf16)


def _prep_compress(pos_k, w1_k, w2_k, pos_v, w1_v, w2_v):
    def w1(w):
        return w.reshape(2, CMP_STRIDE * HEAD_DIM, CMP_HID).astype(bf16)

    def pe(p):
        return p.reshape(2, CMP_STRIDE * HEAD_DIM)

    zl = jnp.zeros_like(w2_k)
    return dict(pek=pe(pos_k), pev=pe(pos_v), w1k=w1(w1_k), w1v=w1(w1_v),
                w2kl=jnp.concatenate([w2_k, zl], 1).astype(bf16), w2kr=jnp.concatenate([zl, w2_k], 1).astype(bf16),
                w2vl=jnp.concatenate([w2_v, zl], 1).astype(bf16), w2vr=jnp.concatenate([zl, w2_v], 1).astype(bf16))


def _prep_w_out(w):
    na = SB_HEADS * HEAD_DIM
    nb = MLA_HEADS * MLA_V
    wa = jnp.concatenate([w[hh * HEAD_DIM:(hh + 1) * HEAD_DIM] for hh in _SB_HEAD_ORDER], axis=0)
    return wa.astype(bf16), w[na:na + nb].astype(bf16), w[na + nb:].astype(bf16)


def _rope_tables(pos):
    half = MLA_ROPE // 2
    inv = ROPE_BASE ** (-jnp.arange(half, dtype=f32) / half)
    ang = pos.astype(f32)[:, None] * inv[None, :]
    pad = ((0, 0), (0, LANES - MLA_ROPE))
    cos = jnp.cos(ang)
    sin = jnp.sin(ang)
    return jnp.pad(jnp.concatenate([cos, cos], 1), pad), jnp.pad(jnp.concatenate([sin, sin], 1), pad)


def _layer_weights(W, l):
    return dict(
        w_in=_prep_w_in(W["w_in"][l]), wq=_prep_w_uq(W["w_uq"][l]), wuk=_prep_w_uk(W["w_uk"][l]),
        wuv=_prep_w_uv(W["w_uv"][l]), gq=W["g_q_lora"][l], gkv=W["g_kv_lora"][l],
        cw=_prep_compress(W["cmp_pos_k"][l], W["cmp_w1_k"][l], W["cmp_w2_k"][l],
                          W["cmp_pos_v"][l], W["cmp_w1_v"][l], W["cmp_w2_v"][l]),
        w_out=_prep_w_out(W["w_out"][l]),
        g_attn_pre=W["g_attn_pre"][l], g_attn_post=W["g_attn_post"][l],
        g_ffn_pre=W["g_ffn_pre"][l], g_ffn_post=W["g_ffn_post"][l],
        w_gate=W["w_gate"][l].astype(bf16), w_up=W["w_up"][l].astype(bf16), w_down=W["w_down"][l].astype(bf16))


def _cache_rows(pf, ckvn, krn):
    c = _COL
    hd = HEAD_DIM
    return dict(sb_k=pf[:, c["sbk"]:c["sbk"] + LANES], sb_v=pf[:, c["sbv"]:c["sbv"] + LANES],
                ckv=ckvn, kr=krn[:, :MLA_ROPE],
                ck=pf[:, c["ckcv"]:c["ckcv"] + hd], cv=pf[:, c["ckcv"] + hd:c["ckcv"] + 2 * hd],
                sk=pf[:, c["sksv"]:c["sksv"] + hd], sv=pf[:, c["sksv"] + hd:c["sksv"] + 2 * hd],
                wk=pf[:, c["wkwv"]:c["wkwv"] + hd], wv=pf[:, c["wkwv"] + hd:c["wkwv"] + 2 * hd])


def _prompt_layer(x, lw, b, t, tabs):
    cos_t, sin_t, rel_bias, tsw, tcmp = tabs
    pf, pb = _proj(x, lw["g_attn_pre"], lw["w_in"])
    qcat, kcat, ckvn, krn = _mla_prep(pf, lw["gq"], lw["gkv"], lw["wq"], lw["wuk"], cos_t, sin_t,
                                      t // min(256, b * t))
    rows = _cache_rows(pf, ckvn, krn)
    nch = t // CMP_STRIDE
    kvc, vkc = _compress(rows["ck"].reshape(b, nch, CMP_STRIDE * HEAD_DIM),
                         rows["cv"].reshape(b, nch, CMP_STRIDE * HEAD_DIM), lw["cw"])
    oa = _sb_prompt(pb, b, t)
    ob = _mla_prompt(qcat, kcat, lw["wuv"], b, t)
    oc = _nsa_prompt(pb, pf, kvc, vkc, rel_bias, tsw, tcmp, b, t)
    wa, wb, wc = lw["w_out"]
    x = _outproj(oa, ob, oc, wa, wb, wc, lw["g_attn_post"], x)
    x = _ffn(x, lw["g_ffn_pre"], lw["w_gate"], lw["w_up"], lw["w_down"], lw["g_ffn_post"])
    return x, rows


def _page_copy(cache_ref, buf_ref, sem_ref, page, slot, p):
    rows, cols = cache_ref.shape[1:]
    if buf_ref.shape[2] == cols:
        dst = buf_ref.at[slot, pl.ds(p * rows, rows)]
    else:
        dst = buf_ref.at[slot, :, pl.ds(p * cols, cols)]
    return pltpu.make_async_copy(cache_ref.at[page], dst, sem_ref.at[slot])


def _gather_pages(pt_ref, caches, bufs, sems, base, npg):
    s = pl.program_id(0)
    slot = s % 2

    def start(seq, sl):
        def body(p, c):
            page = base + pt_ref[seq * npg + p]
            for cache, buf, sem in zip(caches, bufs, sems):
                _page_copy(cache, buf, sem, page, sl, p).start()
            return c
        lax.fori_loop(0, npg, body, 0)

    @pl.when(s == 0)
    def _():
        start(0, 0)

    @pl.when(s + 1 < pl.num_programs(0))
    def _():
        start(s + 1, 1 - slot)

    def wait_body(p, c):
        for cache, buf, sem in zip(caches, bufs, sems):
            _page_copy(cache, buf, sem, 0, slot, p).wait()
        return c
    lax.fori_loop(0, npg, wait_body, 0)
    return slot


def _decode_bias_kernel(rb_ref, tsel_ref, twin_ref, tcmp_ref, tnew_ref, *, past):
    def rowdist(width, fn):
        return jnp.maximum(fn(_iota((1, width), 1)), 0)

    dsel = rowdist(tsel_ref.shape[1], lambda c: tsel_ref.shape[1] - c)
    dwin = rowdist(twin_ref.shape[1], lambda c: twin_ref.shape[1] - c)
    dcmp = rowdist(tcmp_ref.shape[1], lambda n: past - (CMP_STRIDE * n + CMP_LEN - 1))
    lane = _iota((1, LANES), 1)
    for h in range(NSA_HEADS):
        tsel_ref[h:h + 1, :] = _bias_lookup(dsel, rb_ref, h)
        twin_ref[h:h + 1, :] = _bias_lookup(dwin, rb_ref, h)
        tcmp_ref[h:h + 1, :] = _bias_lookup(dcmp, rb_ref, h)
        tnew_ref[h:h + 1, :] = jnp.where(lane < HEAD_DIM, rb_ref[h], rb_ref[(N_BUCKETS - 1) * NSA_HEADS + h])


def _decode_bias_tables(rel_bias, past, ck_sel, pw, nch):
    shp = lambda w: jax.ShapeDtypeStruct((NSA_HEADS, w), f32)
    return pl.pallas_call(
        functools.partial(_decode_bias_kernel, past=past),
        out_shape=(shp(ck_sel), shp(pw), shp(nch), shp(LANES)),
        in_specs=[pl.BlockSpec(memory_space=pltpu.SMEM)],
    )(rel_bias.reshape(-1))


def _sb_decode_kernel(pt_ref, q_ref, kc_ref, vc_ref, o_ref, kbuf, vbuf, sems, *, base, npg, ck):
    slot = _gather_pages(pt_ref, (kc_ref, vc_ref), (kbuf, vbuf), (sems.at[0], sems.at[1]), base, npg)
    past = kbuf.shape[2]
    qg = q_ref[0]
    lane = _iota(qg.shape, 1)
    zero = jnp.zeros_like(qg)
    qm = jnp.concatenate([jnp.where(lane < HEAD_DIM, qg, zero), jnp.where(lane < HEAD_DIM, zero, qg),
                          jnp.zeros((4, LANES), f32)], axis=0).astype(bf16)
    tri = jnp.where(_iota((ck, ck), 0) > _iota((ck, ck), 1), 1.0, 0.0).astype(bf16)
    scale = HEAD_DIM ** -0.5
    nchunk = past // ck

    zs = [_dot(qm, kbuf[slot, :, c * ck:(c + 1) * ck].astype(bf16)) * scale for c in range(nchunk)]
    lrs = [-_softplus(z) for z in zs]
    lr_all = jnp.concatenate(lrs, axis=0)
    hi = lr_all.astype(bf16)
    lo = (lr_all - hi.astype(f32)).astype(bf16)
    inner = _dot(hi, tri) + _dot(lo, tri)
    acc = jnp.zeros((8, LANES), f32)
    run = jnp.zeros((8, 1), f32)
    for c in reversed(range(nchunk)):
        w = jnp.exp(zs[c] + lrs[c] + inner[8 * c:8 * c + 8] + run)
        acc = acc + _dot_nt(w.astype(bf16), vbuf[slot, :, c * ck:(c + 1) * ck].astype(bf16))
        run = run + jnp.sum(lrs[c], axis=-1, keepdims=True)
    o_ref[0] = jnp.where(lane < HEAD_DIM, acc[0:2], acc[2:4])


def _sb_decode(q, cache_k, cache_v, pt, layer):
    s = q.shape[0]
    depth, n_pool, page, _, _ = cache_k.shape
    npg = pt.shape[1]
    past = npg * page
    ck = min(512, past)
    ck2 = lambda c: jnp.transpose(c, (0, 1, 3, 4, 2)).reshape(depth * n_pool, LANES, page)
    grid_spec = pltpu.PrefetchScalarGridSpec(
        num_scalar_prefetch=1, grid=(s,),
        in_specs=[pl.BlockSpec((1, 2, LANES), lambda i, pt_: (i, 0, 0)),
                  pl.BlockSpec(memory_space=pl.ANY), pl.BlockSpec(memory_space=pl.ANY)],
        out_specs=pl.BlockSpec((1, 2, LANES), lambda i, pt_: (i, 0, 0)),
        scratch_shapes=[pltpu.VMEM((2, LANES, past), f32), pltpu.VMEM((2, LANES, past), f32),
                        pltpu.SemaphoreType.DMA((2, 2))])
    out = pl.pallas_call(
        functools.partial(_sb_decode_kernel, base=layer * n_pool, npg=npg, ck=ck),
        grid_spec=grid_spec,
        out_shape=jax.ShapeDtypeStruct((s, 2, LANES), f32),
        compiler_params=_cparams(("arbitrary",)),
    )(pt.reshape(-1), q.reshape(s, 2, LANES), ck2(cache_k), ck2(cache_v))
    return out.reshape(s, 2 * LANES).astype(bf16)


def _mla_decode_kernel(pt_ref, q_ref, kn_ref, wuv_ref, cc_ref, rc_ref, o_ref, cbuf, rbuf, sems, *, base, npg, ck):
    slot = _gather_pages(pt_ref, (cc_ref, rc_ref), (cbuf, rbuf), (sems.at[0], sems.at[1]), base, npg)
    past = cbuf.shape[1]
    nh = MLA_HEADS
    q = jnp.concatenate([q_ref[0], jnp.zeros((8 - nh, 2 * LANES), f32)], axis=0)
    qlat = q[:, :LANES].astype(bf16)
    qrope = q[:, LANES:LANES + MLA_ROPE].astype(bf16)
    scale = (MLA_NOPE + MLA_ROPE) ** -0.5

    kn = kn_ref[0]
    s_new = jnp.sum(q * kn, axis=-1, keepdims=True) * scale
    nchunk = past // ck
    lat = [cbuf[slot, c * ck:(c + 1) * ck, :].astype(bf16) for c in range(nchunk)]
    ss = [(_dot_nt(qlat, lat[c]) + _dot(qrope, rbuf[slot, :, c * ck:(c + 1) * ck].astype(bf16))) * scale
          for c in range(nchunk)]
    m = s_new
    for s in ss:
        m = jnp.maximum(m, jnp.max(s, axis=-1, keepdims=True))
    p_new = jnp.exp(s_new - m)
    l = p_new
    acc = p_new * kn[:, :LANES]
    for c in range(nchunk):
        p = jnp.exp(ss[c] - m)
        l = l + jnp.sum(p, axis=-1, keepdims=True)
        acc = acc + _dot(p.astype(bf16), lat[c])
    olat = (acc / l).astype(bf16)
    out = _dot(olat[0:1], wuv_ref[0])
    for h in range(1, nh):
        out += _dot(olat[h:h + 1], wuv_ref[h])
    o_ref[0] = out


def _mla_decode(qcat, kcat, wuv, cache_c, cache_r, pt, layer):
    s = qcat.shape[0]
    depth, n_pool, page, _ = cache_c.shape
    npg = pt.shape[1]
    past = npg * page
    ck = min(1024, past)
    grid_spec = pltpu.PrefetchScalarGridSpec(
        num_scalar_prefetch=1, grid=(s,),
        in_specs=[pl.BlockSpec((1, MLA_HEADS, 2 * LANES), lambda i, pt_: (i, 0, 0)),
                  pl.BlockSpec((1, 1, 2 * LANES), lambda i, pt_: (i, 0, 0)),
                  pl.BlockSpec(wuv.shape, lambda i, pt_: (0, 0, 0)),
                  pl.BlockSpec(memory_space=pl.ANY), pl.BlockSpec(memory_space=pl.ANY)],
        out_specs=pl.BlockSpec((1, 1, MLA_HEADS * MLA_V), lambda i, pt_: (i, 0, 0)),
        scratch_shapes=[pltpu.VMEM((2, past, MLA_KV_LORA), f32), pltpu.VMEM((2, MLA_ROPE, past), f32),
                        pltpu.SemaphoreType.DMA((2, 2))])
    out = pl.pallas_call(
        functools.partial(_mla_decode_kernel, base=layer * n_pool, npg=npg, ck=ck),
        grid_spec=grid_spec,
        out_shape=jax.ShapeDtypeStruct((s, 1, MLA_HEADS * MLA_V), f32),
        compiler_params=_cparams(("arbitrary",)),
    )(pt.reshape(-1), qcat.astype(f32).reshape(s, MLA_HEADS, 2 * LANES), kcat.astype(f32).reshape(s, 1, 2 * LANES),
      wuv, cache_c.reshape(depth * n_pool, page, MLA_KV_LORA),
      jnp.swapaxes(cache_r, 2, 3).reshape(depth * n_pool, MLA_ROPE, page))
    return out.reshape(s, MLA_HEADS * MLA_V).astype(bf16)


def _softmax_parts(s, mask):
    sm = s if mask is None else jnp.where(mask, s, NEG_INF)
    m = jnp.max(sm, axis=-1, keepdims=True)
    p = jnp.exp(s - m)
    if mask is not None:
        p = jnp.where(mask, p, 0.0)
    return m, p


def _nsa_decode_kernel(pt_ref, q_ref, misc_ref, snew_ref, wnew_ref, wink_ref, winv_ref,
                       pek_ref, pev_ref, w1k_ref, w1v_ref, w2kl_ref, w2kr_ref, w2vl_ref, w2vr_ref,
                       amat_ref, tsel_ref, twin_ref, tcmp_ref, tnew_ref,
                       ckc_ref, cvc_ref, skc_ref, svc_ref, o_ref,
                       ckbuf, cvbuf, skbuf, svbuf, sems, *, base, npg, ck, k_top):
    slot = _gather_pages(pt_ref, (ckc_ref, cvc_ref, skc_ref, svc_ref), (ckbuf, cvbuf, skbuf, svbuf),
                         tuple(sems.at[n] for n in range(4)), base, npg)
    past = skbuf.shape[2]
    nh = NSA_HEADS
    hd = HEAD_DIM
    scale = hd ** -0.5
    q = q_ref[0]
    qb = q.astype(bf16)
    qpad = jnp.concatenate([q, jnp.zeros_like(q)], axis=1).astype(bf16)
    rowi = _iota((nh, LANES), 0)
    lanei = _iota((nh, LANES), 1)
    gates = 1.0 / (1.0 + jnp.exp(-misc_ref[0]))

    def gate(branch):
        pick = lanei == GATE_LANE + branch * nh + rowi
        return jnp.sum(jnp.where(pick, jnp.broadcast_to(gates, (nh, LANES)), 0.0), axis=-1, keepdims=True)

    kv, vk = _compress_math(ckbuf[slot], cvbuf[slot], pek_ref, pev_ref, w1k_ref, w1v_ref,
                            w2kl_ref, w2kr_ref, w2vl_ref, w2vr_ref)
    nch = kv.shape[0]
    ncol = _iota((nh, nch), 1)
    okc = (CMP_STRIDE * ncol + (CMP_LEN - 1)) <= past
    sc = _dot_nt(qpad, kv.astype(bf16)) * scale + tcmp_ref[...]
    _, p = _softmax_parts(sc, okc)
    l = jnp.sum(p, axis=-1, keepdims=True)
    pc = p * jnp.where(l > 0.0, 1.0 / l, 0.0)
    oc = _dot(pc.astype(bf16), vk.astype(bf16))[:, :hd]
    imp = jnp.broadcast_to(jnp.sum(pc, axis=0, keepdims=True), (nh, nch))

    amat = amat_ref[...]
    nsc = amat.shape[1]
    hi, mid, lo = _split3(imp)
    p_slc = (_dot(hi, amat) + _dot(mid, amat) + _dot(lo, amat))[0:1]
    jl = _iota((1, nsc), 1)
    cur = past // SEL_BLOCK
    forced = (jl == 0) | (jl == cur) | (jl == cur - 1)
    score = jnp.where(jl <= cur, jnp.where(forced, FORCE_SCORE, p_slc), -1.0)
    ii = _iota((nsc, nsc), 0)
    jj = _iota((nsc, nsc), 1)
    srow = jnp.broadcast_to(score, (nsc, nsc))
    scol = jnp.sum(jnp.where(ii == jj, srow, 0.0), axis=-1, keepdims=True)
    beats = jnp.where(scol > srow, 1.0, jnp.where((scol == srow) & (ii < jj), 1.0, 0.0))
    rank = jnp.sum(beats, axis=0, keepdims=True)
    sel = jnp.where((rank < k_top) & (score >= 0.0), 1.0, 0.0)
    sel8 = jnp.broadcast_to(sel, (nh, nsc)).astype(bf16)

    nchunk = past // ck
    bpc = ck // SEL_BLOCK
    jb = _iota((nsc, ck), 0)
    cb = _iota((nsc, ck), 1) // SEL_BLOCK
    far = tnew_ref[:, hd:hd + 1]

    masks, ss = [], []
    m = jnp.full((nh, 1), NEG_INF, f32)
    for c in range(nchunk):
        emat = jnp.where(jb == c * bpc + cb, 1.0, 0.0).astype(bf16)
        mask = _dot(sel8, emat) > 0.5
        bias = tsel_ref[...] if c == nchunk - 1 else far
        s = _dot(qb, skbuf[slot, :, c * ck:(c + 1) * ck].astype(bf16)) * scale + bias
        m = jnp.maximum(m, jnp.max(jnp.where(mask, s, NEG_INF), axis=-1, keepdims=True))
        masks.append(mask)
        ss.append(s)
    l = jnp.zeros((nh, 1), f32)
    acc = jnp.zeros((nh, hd), f32)
    for c in range(nchunk):
        p = jnp.where(masks[c], jnp.exp(ss[c] - m), 0.0)
        l = l + jnp.sum(p, axis=-1, keepdims=True)
        acc = acc + _dot_nt(p.astype(bf16), svbuf[slot, :, c * ck:(c + 1) * ck].astype(bf16))

    def add_own(m, l, acc, k_new, v_new, valid):
        s_new = jnp.sum(q * k_new, axis=-1, keepdims=True) * scale + tnew_ref[:, 0:1]
        s_new = jnp.where(valid, s_new, NEG_INF)
        m_f = jnp.maximum(m, s_new)
        p_new = jnp.where(valid, jnp.exp(s_new - m_f), 0.0)
        a = jnp.exp(m - m_f)
        return (a * acc + p_new * v_new) / (a * l + p_new)

    round_b = lambda x: x.astype(bf16).astype(f32)
    snew = round_b(snew_ref[0])
    own_sel = jnp.sum(jnp.where(jl == cur, sel, 0.0), axis=-1, keepdims=True) > 0.5
    osel = add_own(m, l, acc, snew[:, :hd], snew[:, hd:], own_sel)

    kw = wink_ref[...].astype(bf16)
    vw = winv_ref[...].astype(bf16)
    pw = kw.shape[1]
    s = _dot(qb, kw) * scale + twin_ref[...]
    inwin = _iota((nh, pw), 1) > pw - WINDOW
    m, p = _softmax_parts(s, inwin)
    wnew = round_b(wnew_ref[0])
    ow = add_own(m, jnp.sum(p, axis=-1, keepdims=True), _dot_nt(p.astype(bf16), vw), wnew[:, :hd], wnew[:, hd:], True)

    o_ref[0] = gate(0) * oc + gate(1) * osel + gate(2) * ow


def _sel_matrix_decode(nch, n_cmp, ncols):
    a = np.zeros((nch, ncols), np.float32)
    for n in range(n_cmp):
        a[n, n // CH_PER_SEL] += 0.5
        a[n, (n + 1) // CH_PER_SEL] += 0.5
    return a


def _nsa_decode(pf, cw, dtabs, caches, win_k, win_v, pt, layer):
    cache_ck, cache_cv, cache_sk, cache_sv = caches
    s = pf.shape[0]
    depth, n_pool, page, hd = cache_ck.shape
    npg = pt.shape[1]
    past = npg * page
    nch = (past + 1) // CMP_STRIDE
    n_sel = -(-(past + 1) // SEL_BLOCK)
    nsc = -(-n_sel // LANES) * LANES
    ck = min(1024, past)
    pw = win_k.shape[2]
    amat = jnp.asarray(_sel_matrix_decode(nch, nch - 1, nsc), bf16)
    tsel, twin, tcmp, tnew = dtabs
    c = _COL
    q = pf[:, c["nq"]:c["nq"] + NSA_HEADS * hd].reshape(s, NSA_HEADS, hd)
    row3 = lambda name: pf[:, c[name]:c[name] + LANES].reshape(s, 1, LANES)
    seq3 = lambda shape: pl.BlockSpec((1,) + shape, lambda i, pt_: (i, 0, 0))
    fixed = lambda a: pl.BlockSpec(a.shape, lambda i, pt_, n=a.ndim: (0,) * n)
    chunks_per_page = page // CMP_STRIDE
    grid_spec = pltpu.PrefetchScalarGridSpec(
        num_scalar_prefetch=1, grid=(s,),
        in_specs=[seq3((NSA_HEADS, hd)), seq3((1, LANES)), seq3((1, LANES)), seq3((1, LANES)),
                  pl.BlockSpec((None, None, hd, pw), lambda i, pt_: (layer, i, 0, 0)),
                  pl.BlockSpec((None, None, hd, pw), lambda i, pt_: (layer, i, 0, 0))]
                 + [fixed(a) for a in _compress_args(cw)] + [fixed(a) for a in (amat, tsel, twin, tcmp, tnew)]
                 + [pl.BlockSpec(memory_space=pl.ANY)] * 4,
        out_specs=seq3((NSA_HEADS, hd)),
        scratch_shapes=[pltpu.VMEM((2, nch, CMP_STRIDE * hd), f32), pltpu.VMEM((2, nch, CMP_STRIDE * hd), f32),
                        pltpu.VMEM((2, hd, past), f32), pltpu.VMEM((2, hd, past), f32),
                        pltpu.SemaphoreType.DMA((4, 2))])
    chunked = lambda a: a.reshape(depth * n_pool, chunks_per_page, CMP_STRIDE * hd)
    paged = lambda a: jnp.swapaxes(a, 2, 3).reshape(depth * n_pool, hd, page)
    out = pl.pallas_call(
        functools.partial(_nsa_decode_kernel, base=layer * n_pool, npg=npg, ck=ck, k_top=min(SEL_TOPK, n_sel)),
        grid_spec=grid_spec,
        out_shape=jax.ShapeDtypeStruct((s, NSA_HEADS, hd), f32),
        compiler_params=_cparams(("arbitrary",)),
    )(pt.reshape(-1), q, row3("misc"), row3("sksv"), row3("wkwv"), jnp.swapaxes(win_k, 2, 3), jnp.swapaxes(win_v, 2, 3),
      *_compress_args(cw), amat, tsel, twin, tcmp, tnew,
      chunked(cache_ck), chunked(cache_cv), paged(cache_sk), paged(cache_sv))
    return out.reshape(s, NSA_HEADS * hd).astype(bf16)


def _sample_layer(x, lw, tabs, caches, pt, layer):
    cos_t, sin_t, dtabs = tabs
    (c_sbk, c_sbv, c_ckv, c_kr, c_ck, c_cv, c_sk, c_sv, win_k, win_v) = caches
    pf, _ = _proj(x, lw["g_attn_pre"], lw["w_in"])
    qcat, kcat, ckvn, krn = _mla_prep(pf, lw["gq"], lw["gkv"], lw["wq"], lw["wuk"], cos_t, sin_t, 1)
    rows = _cache_rows(pf, ckvn, krn)
    oa = _sb_decode(pf[:, _COL["sbq"]:_COL["sbq"] + 2 * LANES], c_sbk, c_sbv, pt, layer)
    ob = _mla_decode(qcat, kcat, lw["wuv"], c_ckv, c_kr, pt, layer)
    oc = _nsa_decode(pf, lw["cw"], dtabs, (c_ck, c_cv, c_sk, c_sv), win_k, win_v, pt, layer)
    wa, wb, wc = lw["w_out"]
    x = _outproj(oa, ob, oc, wa, wb, wc, lw["g_attn_post"], x)
    x = _ffn(x, lw["g_ffn_pre"], lw["w_gate"], lw["w_up"], lw["w_down"], lw["g_ffn_post"])
    keep = min(WINDOW, win_k.shape[2] + 1)
    rows["wk"] = jnp.concatenate([win_k[layer], rows["wk"][:, None, :]], axis=1)[:, -keep:]
    rows["wv"] = jnp.concatenate([win_v[layer], rows["wv"][:, None, :]], axis=1)[:, -keep:]
    return x, rows


def kernel(x_prompt, x_sample, cache_sb_k, cache_sb_v, cache_mla_ckv, cache_mla_kr, cache_nsa_cmp_k, cache_nsa_cmp_v,
           cache_nsa_sel_k, cache_nsa_sel_v, state_nsa_win_k, state_nsa_win_v, page_table,
           w_in, g_q_lora, w_uq, g_kv_lora, w_uk, w_uv, cmp_pos_k, cmp_w1_k, cmp_w2_k, cmp_pos_v, cmp_w1_v, cmp_w2_v,
           rel_bias, w_out, g_attn_pre, g_attn_post, g_ffn_pre, g_ffn_post, w_gate, w_up, w_down):
    W = dict(w_in=w_in, g_q_lora=g_q_lora, w_uq=w_uq, g_kv_lora=g_kv_lora, w_uk=w_uk, w_uv=w_uv,
             cmp_pos_k=cmp_pos_k, cmp_w1_k=cmp_w1_k, cmp_w2_k=cmp_w2_k,
             cmp_pos_v=cmp_pos_v, cmp_w1_v=cmp_w1_v, cmp_w2_v=cmp_w2_v,
             w_out=w_out, g_attn_pre=g_attn_pre, g_attn_post=g_attn_post,
             g_ffn_pre=g_ffn_pre, g_ffn_post=g_ffn_post, w_gate=w_gate, w_up=w_up, w_down=w_down)
    depth = w_in.shape[0]
    b, t, d = x_prompt.shape
    s = x_sample.shape[0]
    page = cache_sb_k.shape[2]
    past = page_table.shape[1] * page
    lws = [_layer_weights(W, l) for l in range(depth)]

    cos_p, sin_p = _rope_tables(jnp.arange(t, dtype=jnp.int32))
    tsw, tcmp = _bias_tables(rel_bias, t // CMP_STRIDE)
    y = x_prompt.reshape(b * t, d)
    rows_p = []
    for l in range(depth):
        y, rows = _prompt_layer(y, lws[l], b, t, (cos_p, sin_p, rel_bias, tsw, tcmp))
        rows_p.append(rows)
    y_prompt = y.reshape(b, t, d)

    cos_s, sin_s = _rope_tables(jnp.full((s,), past, jnp.int32))
    dtabs = _decode_bias_tables(rel_bias, past, min(1024, past), state_nsa_win_k.shape[2], (past + 1) // CMP_STRIDE)
    caches = (cache_sb_k, cache_sb_v, cache_mla_ckv, cache_mla_kr, cache_nsa_cmp_k, cache_nsa_cmp_v,
              cache_nsa_sel_k, cache_nsa_sel_v, state_nsa_win_k, state_nsa_win_v)
    y = x_sample.reshape(s, d)
    rows_s = []
    for l in range(depth):
        y, rows = _sample_layer(y, lws[l], (cos_s, sin_s, dtabs), caches, page_table, l)
        rows_s.append(rows)
    y_sample = y.reshape(s, 1, d)

    keep_p = min(WINDOW, t)

    def stack(rows_all, name, shape, tail=None):
        parts = [r[name].reshape(shape) for r in rows_all]
        if tail is not None:
            parts = [p[:, -tail:] for p in parts]
        return jnp.stack(parts, axis=0)

    def outputs(rows_all, n, tt, win_tail):
        kvh = (n, tt, SB_KV_HEADS, HEAD_DIM)
        return (stack(rows_all, "sb_k", kvh), stack(rows_all, "sb_v", kvh),
                stack(rows_all, "ckv", (n, tt, MLA_KV_LORA)), stack(rows_all, "kr", (n, tt, MLA_ROPE)),
                stack(rows_all, "ck", (n, tt, HEAD_DIM)), stack(rows_all, "cv", (n, tt, HEAD_DIM)),
                stack(rows_all, "sk", (n, tt, HEAD_DIM)), stack(rows_all, "sv", (n, tt, HEAD_DIM)),
                stack(rows_all, "wk", (n, -1, HEAD_DIM), win_tail), stack(rows_all, "wv", (n, -1, HEAD_DIM), win_tail))

    return (y_prompt, y_sample) + outputs(rows_p, b, t, keep_p) + outputs(rows_s, s, 1, None)
```

```python
import functools
import math

import numpy as np
import jax
import jax.numpy as jnp
from jax import lax
from jax.experimental import pallas as pl
from jax.experimental.pallas import tpu as pltpu

f32 = jnp.float32
bf16 = jnp.bfloat16

HEAD_DIM = 64
SB_HEADS = 4
SB_KV_HEADS = 2
MLA_HEADS = 4
MLA_Q_LORA = 256
MLA_KV_LORA = 128
MLA_NOPE = 64
MLA_ROPE = 32
MLA_V = 64
ROPE_BASE = 10000.0
NSA_HEADS = 8
CMP_STRIDE = 16
CMP_LEN = 2 * CMP_STRIDE
CMP_HID = 256
SEL_BLOCK = 64
CH_PER_SEL = SEL_BLOCK // CMP_STRIDE
SEL_TOPK = 16
WINDOW = 512
N_BUCKETS = 32
MAX_DISTANCE = 128
RMS_EPS = 1e-6
NEG_INF = -1e30
FORCE_SCORE = 1e4

LOG2E = math.log2(math.e)
LANES = 128
VMEM_LIMIT = 56 * 1024 * 1024

_COL = dict(nq=0, sbq=512, cq=768, sbk=1024, sbv=1152, ckv=1280, ckcv=1408, sksv=1536, svsk=1664,
            wkwv=1792, wvwk=1920, misc=2048, rot=2176)
NP = 2304
GATE_LANE = MLA_ROPE


def _cparams(sem):
    return pltpu.CompilerParams(dimension_semantics=sem, vmem_limit_bytes=VMEM_LIMIT)


def _rms(x, g):
    return x * lax.rsqrt(jnp.mean(x * x, axis=-1, keepdims=True) + RMS_EPS) * g


def _dot(a, b):
    return jnp.dot(a, b, preferred_element_type=f32)


def _dot_nt(a, b):
    return lax.dot_general(a, b, (((1,), (1,)), ((), ())), preferred_element_type=f32)


def _iota(shape, dim):
    return lax.broadcasted_iota(jnp.int32, shape, dim)


def _t5_thresholds():
    exact = N_BUCKETS // 2
    d = np.arange(0, 2 * MAX_DISTANCE + 1)
    ratio = np.log(np.maximum(d, 1).astype(np.float32) / exact) / math.log(MAX_DISTANCE / exact)
    large = np.minimum(exact + (ratio * (N_BUCKETS - exact)).astype(np.int32), N_BUCKETS - 1)
    bucket = np.where(d < exact, d, large)
    assert np.all(np.diff(bucket) >= 0)
    return [int(np.argmax(bucket >= b)) for b in range(1, N_BUCKETS)]


_T5_THR = _t5_thresholds()


def _bias_lookup(dist, rb_ref, h):
    val = jnp.full(dist.shape, rb_ref[h], f32)
    for b in range(1, N_BUCKETS):
        val = jnp.where(dist >= _T5_THR[b - 1], rb_ref[b * NSA_HEADS + h], val)
    return val


def _bias_tables_kernel(rb_ref, tsw_ref, tcmp_ref, *, wc):
    r = _iota((LANES, LANES), 0)
    c = _iota((LANES, LANES), 1)
    rc = _iota((LANES, wc), 0)
    cc = _iota((LANES, wc), 1)
    dcmp = jnp.maximum(rc - CMP_STRIDE * cc + (LANES - CMP_STRIDE + 1), 0)
    for h in range(NSA_HEADS):
        far = rb_ref[(N_BUCKETS - 1) * NSA_HEADS + h]
        d0 = r - c
        tsw_ref[h, 0] = jnp.where(d0 >= 0, _bias_lookup(jnp.maximum(d0, 0), rb_ref, h) * LOG2E, NEG_INF)
        tsw_ref[h, 1] = _bias_lookup(d0 + LANES, rb_ref, h) * LOG2E
        tsw_ref[h, 2] = jnp.full((LANES, LANES), far * LOG2E, f32)
        tcmp_ref[h] = jnp.where(cc < 2 * LANES // CMP_STRIDE, _bias_lookup(dcmp, rb_ref, h), far)


def _bias_tables(rel_bias, wc):
    return pl.pallas_call(
        functools.partial(_bias_tables_kernel, wc=wc),
        out_shape=(jax.ShapeDtypeStruct((NSA_HEADS, 3, LANES, LANES), f32),
                   jax.ShapeDtypeStruct((NSA_HEADS, LANES, wc), f32)),
        in_specs=[pl.BlockSpec(memory_space=pltpu.SMEM)],
    )(rel_bias.reshape(-1))


def _proj_kernel(x_ref, g_ref, w_ref, of_ref, ob_ref):
    h = _rms(x_ref[...], g_ref[...]).astype(bf16)
    y = _dot(h, w_ref[...])
    of_ref[...] = y
    ob_ref[...] = y.astype(bf16)


def _proj(x, g, w):
    m, d = x.shape
    tm = min(256, m)
    return pl.pallas_call(
        _proj_kernel,
        grid=(m // tm,),
        in_specs=[pl.BlockSpec((tm, d), lambda i: (i, 0)),
                  pl.BlockSpec((1, d), lambda i: (0, 0)),
                  pl.BlockSpec((d, NP), lambda i: (0, 0))],
        out_specs=(pl.BlockSpec((tm, NP), lambda i: (i, 0)),
                   pl.BlockSpec((tm, NP), lambda i: (i, 0))),
        out_shape=(jax.ShapeDtypeStruct((m, NP), f32), jax.ShapeDtypeStruct((m, NP), bf16)),
        compiler_params=_cparams(("parallel",)),
    )(x, g.reshape(1, d), w)


def _outproj_kernel(oa_ref, ob_ref, oc_ref, wa_ref, wb_ref, wc_ref, g_ref, x_ref, o_ref):
    y = _dot(oa_ref[...], wa_ref[...]) + _dot(ob_ref[...], wb_ref[...]) + _dot(oc_ref[...], wc_ref[...])
    o_ref[...] = x_ref[...] + _rms(y, g_ref[...])


def _outproj(oa, ob, oc, wa, wb, wc, g, x):
    m, d = x.shape
    tm = min(512, m)
    row = lambda i: (i, 0)
    fixed = lambda i: (0, 0)
    return pl.pallas_call(
        _outproj_kernel,
        grid=(m // tm,),
        in_specs=[pl.BlockSpec((tm, oa.shape[1]), row), pl.BlockSpec((tm, ob.shape[1]), row),
                  pl.BlockSpec((tm, oc.shape[1]), row),
                  pl.BlockSpec(wa.shape, fixed), pl.BlockSpec(wb.shape, fixed), pl.BlockSpec(wc.shape, fixed),
                  pl.BlockSpec((1, d), fixed), pl.BlockSpec((tm, d), row)],
        out_specs=pl.BlockSpec((tm, d), row),
        out_shape=jax.ShapeDtypeStruct((m, d), f32),
        compiler_params=_cparams(("parallel",)),
    )(oa, ob, oc, wa, wb, wc, g.reshape(1, d), x)


def _ffn_kernel(x_ref, gpre_ref, wg_ref, wu_ref, wd_ref, gpost_ref, o_ref, hf_ref, acc_ref):
    j = pl.program_id(1)

    @pl.when(j == 0)
    def _():
        hf_ref[...] = _rms(x_ref[...], gpre_ref[...]).astype(bf16)
        acc_ref[...] = jnp.zeros_like(acc_ref)

    hf = hf_ref[...]
    a = _dot(hf, wg_ref[...])
    u = _dot(hf, wu_ref[...])
    act = (a / (1.0 + jnp.exp(-a))) * u
    acc_ref[...] += _dot(act.astype(bf16), wd_ref[...])

    @pl.when(j == pl.num_programs(1) - 1)
    def _():
        o_ref[...] = x_ref[...] + _rms(acc_ref[...], gpost_ref[...])


def _ffn(x, gpre, wg, wu, wd, gpost):
    m, d = x.shape
    ff = wg.shape[1]
    tm = min(512, m)
    tf = ff // 2 if (ff // 2) % LANES == 0 else ff
    return pl.pallas_call(
        _ffn_kernel,
        grid=(m // tm, ff // tf),
        in_specs=[pl.BlockSpec((tm, d), lambda i, j: (i, 0)),
                  pl.BlockSpec((1, d), lambda i, j: (0, 0)),
                  pl.BlockSpec((d, tf), lambda i, j: (0, j)),
                  pl.BlockSpec((d, tf), lambda i, j: (0, j)),
                  pl.BlockSpec((tf, d), lambda i, j: (j, 0)),
                  pl.BlockSpec((1, d), lambda i, j: (0, 0))],
        out_specs=pl.BlockSpec((tm, d), lambda i, j: (i, 0)),
        out_shape=jax.ShapeDtypeStruct((m, d), f32),
        scratch_shapes=[pltpu.VMEM((tm, d), bf16), pltpu.VMEM((tm, d), f32)],
        compiler_params=_cparams(("parallel", "arbitrary")),
    )(x, gpre.reshape(1, d), wg, wu, wd, gpost.reshape(1, d))


def _mla_prep_kernel(cq_ref, ckv_ref, misc_ref, rot_ref, gq_ref, gkv_ref, wq_ref, wuk_ref, cos_ref, sin_ref,
                     qcat_ref, kcat_ref, ckvn_ref, krn_ref):
    cos = cos_ref[...]
    sin = sin_ref[...]
    cq = _rms(cq_ref[...], gq_ref[...]).astype(bf16)
    qa = _dot(cq, wq_ref[...])
    nh = MLA_HEADS
    for h in range(nh):
        nope = qa[:, h * LANES:(h + 1) * LANES].astype(bf16)
        qlat = _dot(nope, wuk_ref[h])
        rope = qa[:, (nh + h) * LANES:(nh + h + 1) * LANES] * cos + qa[:, (2 * nh + h) * LANES:(2 * nh + h + 1) * LANES] * sin
        qcat_ref[:, 2 * h * LANES:(2 * h + 1) * LANES] = qlat.astype(bf16)
        qcat_ref[:, (2 * h + 1) * LANES:(2 * h + 2) * LANES] = rope.astype(bf16)
    ckv = _rms(ckv_ref[...], gkv_ref[...])
    kr = misc_ref[...] * cos + rot_ref[...] * sin
    ckvn_ref[...] = ckv
    krn_ref[...] = kr
    kcat_ref[:, :LANES] = ckv.astype(bf16)
    lane = _iota(kr.shape, 1)
    kcat_ref[:, LANES:] = jnp.where(lane < MLA_ROPE, kr, 1.0).astype(bf16)


def _mla_prep(pf, gq, gkv, wq, wuk, cos_t, sin_t, n_pos_blocks):
    m = pf.shape[0]
    tm = min(256, m)
    cb = lambda name, w: _COL[name] // w
    tab = lambda i: (i % n_pos_blocks, 0)
    fixed2 = lambda i: (0, 0)
    return pl.pallas_call(
        _mla_prep_kernel,
        grid=(m // tm,),
        in_specs=[pl.BlockSpec((tm, MLA_Q_LORA), lambda i: (i, cb("cq", MLA_Q_LORA))),
                  pl.BlockSpec((tm, LANES), lambda i: (i, cb("ckv", LANES))),
                  pl.BlockSpec((tm, LANES), lambda i: (i, cb("misc", LANES))),
                  pl.BlockSpec((tm, LANES), lambda i: (i, cb("rot", LANES))),
                  pl.BlockSpec((1, MLA_Q_LORA), fixed2), pl.BlockSpec((1, MLA_KV_LORA), fixed2),
                  pl.BlockSpec(wq.shape, fixed2), pl.BlockSpec(wuk.shape, lambda i: (0, 0, 0)),
                  pl.BlockSpec((tm, LANES), tab), pl.BlockSpec((tm, LANES), tab)],
        out_specs=(pl.BlockSpec((tm, 2 * LANES * MLA_HEADS), lambda i: (i, 0)),
                   pl.BlockSpec((tm, 2 * LANES), lambda i: (i, 0)),
                   pl.BlockSpec((tm, LANES), lambda i: (i, 0)),
                   pl.BlockSpec((tm, LANES), lambda i: (i, 0))),
        out_shape=(jax.ShapeDtypeStruct((m, 2 * LANES * MLA_HEADS), bf16),
                   jax.ShapeDtypeStruct((m, 2 * LANES), bf16),
                   jax.ShapeDtypeStruct((m, LANES), f32),
                   jax.ShapeDtypeStruct((m, LANES), f32)),
        compiler_params=_cparams(("parallel",)),
    )(pf, pf, pf, pf, gq.reshape(1, -1), gkv.reshape(1, -1), wq, wuk, cos_t, sin_t)


def _compress_math(chk, chv, pek_ref, pev_ref, w1k_ref, w1v_ref, w2kl_ref, w2kr_ref, w2vl_ref, w2vr_ref):
    def halves(ch, pe_ref, w1_ref):
        return (_dot((ch + pe_ref[0:1, :]).astype(bf16), w1_ref[0]),
                _dot((ch + pe_ref[1:2, :]).astype(bf16), w1_ref[1]))

    return _compress_finish(halves(chk, pek_ref, w1k_ref), halves(chv, pev_ref, w1v_ref),
                            w2kl_ref, w2kr_ref, w2vl_ref, w2vr_ref)


def _compress_finish(halves_k, halves_v, w2kl_ref, w2kr_ref, w2vl_ref, w2vr_ref):
    def hidden(first, second):
        pre = first + pltpu.roll(second, first.shape[0] - 1, axis=0)
        return (pre / (1.0 + jnp.exp(-pre))).astype(bf16)

    hk = hidden(*halves_k)
    hv = hidden(*halves_v)
    kv = _dot(hk, w2kl_ref[...]) + _dot(hv, w2vr_ref[...])
    vk = _dot(hv, w2vl_ref[...]) + _dot(hk, w2kr_ref[...])
    return kv, vk


def _strided_halves(buf_ref, slot, nch, pe_ref, w1_ref):
    first = jnp.zeros((nch, CMP_HID), f32)
    second = jnp.zeros((nch, CMP_HID), f32)
    for s in range(0, CMP_STRIDE, 2):
        xs = jnp.concatenate([buf_ref[slot, pl.ds(s, nch, stride=CMP_STRIDE), :],
                              buf_ref[slot, pl.ds(s + 1, nch, stride=CMP_STRIDE), :]], axis=1)
        cols = slice(s * HEAD_DIM, (s + 2) * HEAD_DIM)
        first = first + _dot((xs + pe_ref[0:1, cols]).astype(bf16), w1_ref[0, cols, :])
        second = second + _dot((xs + pe_ref[1:2, cols]).astype(bf16), w1_ref[1, cols, :])
    return first, second


def _compress_kernel(chk_ref, chv_ref, pek_ref, pev_ref, w1k_ref, w1v_ref, w2kl_ref, w2kr_ref, w2vl_ref, w2vr_ref,
                     kv_ref, vk_ref):
    kv, vk = _compress_math(chk_ref[0], chv_ref[0], pek_ref, pev_ref, w1k_ref, w1v_ref,
                            w2kl_ref, w2kr_ref, w2vl_ref, w2vr_ref)
    kv_ref[0] = kv.astype(bf16)
    vk_ref[0] = vk.astype(bf16)


def _compress_specs(cw):
    c2 = lambda *_: (0, 0)
    c3 = lambda *_: (0, 0, 0)
    return [pl.BlockSpec(cw["pek"].shape, c2), pl.BlockSpec(cw["pev"].shape, c2),
            pl.BlockSpec(cw["w1k"].shape, c3), pl.BlockSpec(cw["w1v"].shape, c3),
            pl.BlockSpec(cw["w2kl"].shape, c2), pl.BlockSpec(cw["w2kr"].shape, c2),
            pl.BlockSpec(cw["w2vl"].shape, c2), pl.BlockSpec(cw["w2vr"].shape, c2)]


def _compress_args(cw):
    return (cw["pek"], cw["pev"], cw["w1k"], cw["w1v"], cw["w2kl"], cw["w2kr"], cw["w2vl"], cw["w2vr"])


def _compress(chk, chv, cw):
    b, nch, cl = chk.shape
    blk = pl.BlockSpec((1, nch, cl), lambda i: (i, 0, 0))
    out = pl.BlockSpec((1, nch, LANES), lambda i: (i, 0, 0))
    return pl.pallas_call(
        _compress_kernel,
        grid=(b,),
        in_specs=[blk, blk] + _compress_specs(cw),
        out_specs=(out, out),
        out_shape=(jax.ShapeDtypeStruct((b, nch, LANES), bf16),) * 2,
        compiler_params=_cparams(("parallel",)),
    )(chk, chv, *_compress_args(cw))


def _softplus(z):
    return jnp.maximum(z, 0.0) + jnp.log(1.0 + jnp.exp(-jnp.abs(z)))


def _sb_tile(qm, k_t, v_t, valid, carry, tri_ones):
    tk = z_cols = k_t.shape[0]
    z = _dot_nt(qm, k_t) * (HEAD_DIM ** -0.5)
    lr = -_softplus(z)
    if valid is not None:
        lr = jnp.where(valid, lr, 0.0)
    hi = lr.astype(bf16)
    lo = (lr - hi.astype(f32)).astype(bf16)
    sums = _dot(hi, tri_ones) + _dot(lo, tri_ones)
    reps = z_cols // LANES
    after = sums[:, :tk] + (jnp.concatenate([carry] * reps, axis=1) if reps > 1 else carry)
    w = jnp.exp(z + lr + after)
    if valid is not None:
        w = jnp.where(valid, w, 0.0)
    return _dot(w.astype(bf16), v_t), sums[:, tk:]


def _sb_prompt_kernel(q_ref, k_ref, v_ref, o_ref, acc_ref, car_ref, *, tq, tk):
    i = pl.program_id(1)
    lane = _iota((tq, LANES), 1)
    rk = _iota((tk, tk + LANES), 0)
    ck = _iota((tk, tk + LANES), 1)
    tri_ones = jnp.where((rk > ck) | (ck >= tk), 1.0, 0.0).astype(bf16)
    diag_valid = _iota((tq, tk), 1) < _iota((tq, tk), 0)
    acc_ref[...] = jnp.zeros_like(acc_ref)
    car_ref[...] = jnp.zeros_like(car_ref)
    heads = [(g, h) for g in range(2) for h in range(SB_KV_HEADS)]
    qms = []
    for g, h in heads:
        qp = q_ref[:, g * LANES:(g + 1) * LANES]
        qms.append(jnp.where((lane // HEAD_DIM) == h, qp, jnp.zeros_like(qp)))
    def block(kb, valid):
        k_t = k_ref[pl.ds(pl.multiple_of(kb * tk, tk), tk), :]
        v_t = v_ref[pl.ds(pl.multiple_of(kb * tk, tk), tk), :]
        for n in range(len(heads)):
            pv, tot = _sb_tile(qms[n], k_t, v_t, valid, car_ref[n], tri_ones)
            acc_ref[n] += pv
            car_ref[n] += tot

    block(i, diag_valid)

    def body(t, c):
        block(i - 1 - t, None)
        return c

    lax.fori_loop(0, i, body, 0)
    for g in range(2):
        o_ref[:, g * LANES:(g + 1) * LANES] = jnp.where(lane < HEAD_DIM, acc_ref[2 * g], acc_ref[2 * g + 1]).astype(bf16)


def _sb_prompt(pb, b, t):
    tq = min(256, t)
    tk = tq
    nq = t // tq
    return pl.pallas_call(
        functools.partial(_sb_prompt_kernel, tq=tq, tk=tk),
        grid=(b, nq),
        in_specs=[pl.BlockSpec((tq, 2 * LANES), lambda bb, i: (bb * nq + i, _COL["sbq"] // (2 * LANES))),
                  pl.BlockSpec((t, LANES), lambda bb, i: (bb, _COL["sbk"] // LANES)),
                  pl.BlockSpec((t, LANES), lambda bb, i: (bb, _COL["sbv"] // LANES))],
        out_specs=pl.BlockSpec((tq, 2 * LANES), lambda bb, i: (bb * nq + i, 0)),
        out_shape=jax.ShapeDtypeStruct((b * t, 2 * LANES), bf16),
        scratch_shapes=[pltpu.VMEM((SB_HEADS, tq, LANES), f32), pltpu.VMEM((SB_HEADS, tq, LANES), f32)],
        compiler_params=_cparams(("parallel", "parallel")),
    )(pb, pb, pb)


def _mla_prompt_kernel(q_ref, k_ref, wuv_ref, o_ref, m_ref, acc_ref, *, tq, tk):
    i = pl.program_id(1)
    nh = MLA_HEADS
    qs = jnp.concatenate([q_ref[:, h * 2 * LANES:(h + 1) * 2 * LANES] for h in range(nh)], axis=0)
    scale = (MLA_NOPE + MLA_ROPE) ** -0.5
    qpos = i * tq + (_iota((nh * tq, tk), 0) % tq)
    kcol = _iota((nh * tq, tk), 1)
    m_ref[...] = jnp.full(m_ref.shape, NEG_INF, f32)
    acc_ref[...] = jnp.zeros_like(acc_ref)
    nkb = (i * tq + tq - 1) // tk + 1

    def body(kb, c):
        k_t = k_ref[pl.ds(pl.multiple_of(kb * tk, tk), tk), :]
        s = _dot_nt(qs, k_t) * (scale * LOG2E)
        s = jnp.where((kb * tk + kcol) <= qpos, s, NEG_INF)
        m_old = m_ref[...]
        m_new = jnp.maximum(m_old, jnp.max(s, axis=-1, keepdims=True))
        p = jnp.exp2(s - jnp.concatenate([m_new] * (tk // LANES), axis=1))
        alpha = jnp.exp2(m_old - m_new)
        acc_ref[...] = jnp.concatenate([alpha, alpha], axis=1) * acc_ref[...] + _dot(p.astype(bf16), k_t)
        m_ref[...] = m_new
        return c

    lax.fori_loop(0, nkb, body, 0)
    acc = acc_ref[...]
    olat = (acc[:, :LANES] / acc[:, 2 * LANES - 1:2 * LANES]).astype(bf16)
    out = _dot(olat[0:tq], wuv_ref[0])
    for h in range(1, nh):
        out += _dot(olat[h * tq:(h + 1) * tq], wuv_ref[h])
    o_ref[...] = out.astype(bf16)


def _mla_prompt(qcat, kcat, wuv, b, t):
    tq = min(256, t)
    tk = min(512, t)
    nq = t // tq
    return pl.pallas_call(
        functools.partial(_mla_prompt_kernel, tq=tq, tk=tk),
        grid=(b, nq),
        in_specs=[pl.BlockSpec((tq, qcat.shape[1]), lambda bb, i: (bb * nq + i, 0)),
                  pl.BlockSpec((t, 2 * LANES), lambda bb, i: (bb, 0)),
                  pl.BlockSpec(wuv.shape, lambda bb, i: (0, 0, 0))],
        out_specs=pl.BlockSpec((tq, MLA_HEADS * MLA_V), lambda bb, i: (bb * nq + i, 0)),
        out_shape=jax.ShapeDtypeStruct((b * t, MLA_HEADS * MLA_V), bf16),
        scratch_shapes=[pltpu.VMEM((MLA_HEADS * tq, LANES), f32), pltpu.VMEM((MLA_HEADS * tq, 2 * LANES), f32)],
        compiler_params=_cparams(("parallel", "parallel")),
    )(qcat, kcat, wuv)


def _split3(x):
    hi = x.astype(bf16)
    r1 = x - hi.astype(f32)
    mid = r1.astype(bf16)
    lo = (r1 - mid.astype(f32)).astype(bf16)
    return hi, mid, lo


def _wide_update(s, mask, v_t, m_ref, acc_ref, idx):
    reps = s.shape[1] // LANES
    m_old = m_ref[idx]
    m_new = jnp.maximum(m_old, jnp.max(jnp.where(mask, s, NEG_INF), axis=-1, keepdims=True))
    m_wide = jnp.concatenate([m_new] * reps, axis=1) if reps > 1 else m_new
    p = jnp.where(mask, jnp.exp2(s - m_wide), 0.0)
    acc_ref[idx] = jnp.exp2(m_old - m_new) * acc_ref[idx] + _dot(p.astype(bf16), v_t)
    m_ref[idx] = m_new


def _pair_outputs(acc_ref, gates, branch, lane_lo):
    outs = []
    for p in range(NSA_HEADS // 2):
        pair = []
        for h in (2 * p, 2 * p + 1):
            a = acc_ref[h]
            o = a * pltpu.roll(1.0 / a, HEAD_DIM, axis=1)
            gl = GATE_LANE + branch * NSA_HEADS + h
            pair.append(o * gates[:, gl:gl + 1])
        outs.append(jnp.where(lane_lo, pair[0], pair[1]))
    return outs


def _nsa_prompt_kernel(rb_ref, q_ref, misc_ref, kvs_ref, vks_ref, kvw_ref, vkw_ref, kvc_ref, vkc_ref,
                       tsw_ref, tcmp_ref, amat_ref, o_ref, m_ref, acc_ref, *, tq, n_sel, k_top, tf, nw):
    i = pl.program_id(1)
    nh = NSA_HEADS
    npair = nh // 2
    wc = kvc_ref.shape[1]
    scale = HEAD_DIM ** -0.5
    lane = _iota((tq, LANES), 1)
    lane_lo = lane < HEAD_DIM
    row = _iota((tq, LANES), 0)
    qpos = i * tq + row

    qe, qo = [], []
    for p in range(npair):
        qp = q_ref[:, p * LANES:(p + 1) * LANES]
        qe.append(jnp.where(lane_lo, qp, jnp.zeros_like(qp)))
        qo.append(jnp.where(lane_lo, jnp.zeros_like(qp), qp))
    qe = jnp.concatenate(qe, axis=0)
    qo = jnp.concatenate(qo, axis=0)

    def head_rows(se, so, h):
        src = se if h % 2 == 0 else so
        return src[(h // 2) * tq:(h // 2 + 1) * tq]

    gates = 1.0 / (1.0 + jnp.exp(-misc_ref[...]))

    kvc = kvc_ref[0]
    vkc = vkc_ref[0]
    se = _dot_nt(qe, kvc) * scale
    so = _dot_nt(qo, vkc) * scale
    ncol = _iota((tq, wc), 1)
    qpos_c = i * tq + _iota((tq, wc), 0)
    okc = (CMP_STRIDE * ncol + (CMP_LEN - 1)) <= qpos_c
    shift = (i * (tq // CMP_STRIDE) + wc - (LANES // CMP_STRIDE + 1)) % wc
    imp = jnp.zeros((tq, wc), f32)
    oc = []
    for h in range(nh):
        s = head_rows(se, so, h) + pltpu.roll(tcmp_ref[h], shift, axis=1)
        sm = jnp.where(okc, s, NEG_INF)
        mx = jnp.max(sm, axis=-1, keepdims=True)
        p = jnp.where(okc, jnp.exp(s - mx), 0.0)
        l = jnp.sum(p, axis=-1, keepdims=True)
        pc = p * jnp.where(l > 0.0, 1.0 / l, 0.0)
        imp = imp + pc
        oc.append(_dot(pc.astype(bf16), vkc if h % 2 == 0 else kvc))
    out = []
    for p in range(npair):
        ge = gates[:, GATE_LANE + 2 * p:GATE_LANE + 2 * p + 1]
        go = gates[:, GATE_LANE + 2 * p + 1:GATE_LANE + 2 * p + 2]
        out.append(jnp.where(lane_lo, oc[2 * p] * ge, oc[2 * p + 1] * go))

    amat = amat_ref[...]
    hi, mid, lo = _split3(imp)
    p_slc = _dot(hi, amat) + _dot(mid, amat) + _dot(lo, amat)
    cur = qpos // SEL_BLOCK
    forced = (lane == 0) | (lane == cur) | (lane == cur - 1)
    score = jnp.where(lane <= cur, jnp.where(forced, FORCE_SCORE, p_slc), -1.0)
    st = score.T[0:n_sel]
    jrow = _iota((n_sel, tq), 0)
    rank = jnp.zeros((n_sel, tq), f32)
    for b in range(n_sel):
        rb = st[b:b + 1, :]
        rank = rank + jnp.where(rb > st, 1.0, jnp.where((rb == st) & (jrow > b), 1.0, 0.0))
    sel_t = jnp.where((rank < k_top) & (st >= 0.0), 1.0, 0.0)
    if n_sel < LANES:
        sel_t = jnp.concatenate([sel_t, jnp.zeros((LANES - n_sel, tq), f32)], axis=0)
    sel = sel_t.T.astype(bf16)

    def reset():
        m_ref[...] = jnp.full(m_ref.shape, NEG_INF, f32)
        acc_ref[...] = jnp.zeros_like(acc_ref)

    sc2 = scale * LOG2E
    one_b = jnp.ones((), bf16)
    kb0 = jnp.maximum(i - WINDOW // LANES, 0)
    near0 = pl.multiple_of(kb0 * LANES, LANES)
    sub = nw // LANES

    def attend(kv_ref, vk_ref, off, width, bias_fn, mask):
        kv = kv_ref[pl.ds(off, width), :]
        vk = vk_ref[pl.ds(off, width), :]
        se = _dot_nt(qe, kv) * sc2
        so = _dot_nt(qo, vk) * sc2
        lk = _iota(kv.shape, 1) < HEAD_DIM
        v_even = jnp.where(lk, vk, one_b)
        v_odd = jnp.where(lk, one_b, kv)
        for h in range(nh):
            s = bias_fn(head_rows(se, so, h), h)
            _wide_update(s, mask, v_even if h % 2 == 0 else v_odd, m_ref, acc_ref, h)

    def near_bias(s, h):
        tiles = [tsw_ref[h, jnp.clip(i - kb0 - j, 0, 2)] for j in range(sub)]
        return s + (jnp.concatenate(tiles, axis=1) if sub > 1 else tiles[0])

    qpos_n = i * tq + _iota((tq, nw), 0)
    kpos_n = near0 + _iota((tq, nw), 1)
    causal_n = kpos_n <= qpos_n

    reset()
    jf = _iota((LANES, tf), 0)
    cf = _iota((LANES, tf), 1)
    kcol_f = _iota((tq, tf), 1)

    def far_body(t, c):
        off = pl.multiple_of(t * tf, tf)
        emat = jnp.where(jf == (tf // SEL_BLOCK) * t + cf // SEL_BLOCK, 1.0, 0.0).astype(bf16)
        mask = (_dot(sel, emat) > 0.5) & (off + kcol_f < near0)
        attend(kvs_ref, vks_ref, off, tf, lambda s, h: s + rb_ref[(N_BUCKETS - 1) * nh + h] * LOG2E, mask)
        return c

    lax.fori_loop(0, (near0 + tf - 1) // tf, far_body, 0)
    jn = _iota((LANES, nw), 0)
    cn = _iota((LANES, nw), 1)
    emat = jnp.where(jn == (LANES // SEL_BLOCK) * kb0 + cn // SEL_BLOCK, 1.0, 0.0).astype(bf16)
    attend(kvs_ref, vks_ref, near0, nw, near_bias, (_dot(sel, emat) > 0.5) & causal_n)
    for p, blk in enumerate(_pair_outputs(acc_ref, gates, 1, lane_lo)):
        out[p] = out[p] + blk

    reset()
    attend(kvw_ref, vkw_ref, near0, nw, near_bias, causal_n & (kpos_n > qpos_n - WINDOW))
    for p, blk in enumerate(_pair_outputs(acc_ref, gates, 2, lane_lo)):
        out[p] = out[p] + blk
    for p in range(npair):
        o_ref[:, p * LANES:(p + 1) * LANES] = out[p].astype(bf16)


def _sel_matrix(wc, n_cmp):
    a = np.zeros((wc, LANES), np.float32)
    for n in range(n_cmp):
        a[n, n // CH_PER_SEL] += 0.5
        a[n, (n + 1) // CH_PER_SEL] += 0.5
    return a


def _nsa_prompt(pb, pf, kvc, vkc, rel_bias, tsw, tcmp, b, t):
    tq = LANES
    nq = t // tq
    wc = kvc.shape[1]
    n_sel = t // SEL_BLOCK
    amat = jnp.asarray(_sel_matrix(wc, wc - 1), bf16)
    full = lambda name: pl.BlockSpec((t, LANES), lambda bb, i, n=name: (bb, _COL[n] // LANES))
    tf = min(4 * LANES, t)
    nw = min(WINDOW + LANES, t)
    return pl.pallas_call(
        functools.partial(_nsa_prompt_kernel, tq=tq, n_sel=n_sel, k_top=min(SEL_TOPK, n_sel), tf=tf, nw=nw),
        grid=(b, nq),
        in_specs=[pl.BlockSpec(memory_space=pltpu.SMEM),
                  pl.BlockSpec((tq, NSA_HEADS * HEAD_DIM), lambda bb, i: (bb * nq + i, 0)),
                  pl.BlockSpec((tq, LANES), lambda bb, i: (bb * nq + i, _COL["misc"] // LANES)),
                  full("sksv"), full("svsk"), full("wkwv"), full("wvwk"),
                  pl.BlockSpec((1, wc, LANES), lambda bb, i: (bb, 0, 0)),
                  pl.BlockSpec((1, wc, LANES), lambda bb, i: (bb, 0, 0)),
                  pl.BlockSpec(tsw.shape, lambda bb, i: (0, 0, 0, 0)),
                  pl.BlockSpec(tcmp.shape, lambda bb, i: (0, 0, 0)),
                  pl.BlockSpec(amat.shape, lambda bb, i: (0, 0))],
        out_specs=pl.BlockSpec((tq, NSA_HEADS * HEAD_DIM), lambda bb, i: (bb * nq + i, 0)),
        out_shape=jax.ShapeDtypeStruct((b * t, NSA_HEADS * HEAD_DIM), bf16),
        scratch_shapes=[pltpu.VMEM((NSA_HEADS, tq, LANES), f32), pltpu.VMEM((NSA_HEADS, tq, LANES), f32)],
        compiler_params=_cparams(("parallel", "parallel")),
    )(rel_bias.reshape(-1), pb, pf, pb, pb, pb, pb, kvc, vkc, tsw, tcmp, amat)


_IN_SIZES = (SB_HEADS * HEAD_DIM, SB_KV_HEADS * HEAD_DIM, SB_KV_HEADS * HEAD_DIM,
             MLA_Q_LORA, MLA_KV_LORA, MLA_ROPE,
             NSA_HEADS * HEAD_DIM, HEAD_DIM, HEAD_DIM, HEAD_DIM, HEAD_DIM, HEAD_DIM, HEAD_DIM,
             3 * NSA_HEADS)
_SB_HEAD_ORDER = (0, 2, 1, 3)


def _rot_half_cols(w):
    half = w.shape[-1] // 2
    return jnp.concatenate([-w[..., half:], w[..., :half]], axis=-1)


def _prep_w_in(w):
    offs = np.concatenate([[0], np.cumsum(_IN_SIZES)])
    (sbq, sbk, sbv, cq, ckv, kr, nq, ck, cv, sk, sv, wk, wv, gl) = [w[:, offs[n]:offs[n + 1]] for n in range(len(_IN_SIZES))]
    sbq = jnp.concatenate([sbq[:, hh * HEAD_DIM:(hh + 1) * HEAD_DIM] for hh in _SB_HEAD_ORDER], axis=1)
    z = lambda n: jnp.zeros((w.shape[0], n), w.dtype)
    cols = [nq, sbq, cq, sbk, sbv, ckv, ck, cv, sk, sv, sv, sk, wk, wv, wv, wk,
            kr, gl, z(LANES - MLA_ROPE - 3 * NSA_HEADS), _rot_half_cols(kr), z(LANES - MLA_ROPE)]
    out = jnp.concatenate(cols, axis=1)
    assert out.shape[1] == NP
    return out.astype(bf16)


def _prep_w_uq(w_uq):
    d = w_uq.shape[0]
    per = w_uq.reshape(d, MLA_HEADS, MLA_NOPE + MLA_ROPE)
    nope = jnp.pad(per[:, :, :MLA_NOPE], ((0, 0), (0, 0), (0, LANES - MLA_NOPE)))
    rope = per[:, :, MLA_NOPE:]
    rpad = ((0, 0), (0, 0), (0, LANES - MLA_ROPE))
    blocks = [nope, jnp.pad(rope, rpad), jnp.pad(_rot_half_cols(rope), rpad)]
    return jnp.concatenate([x.reshape(d, MLA_HEADS * LANES) for x in blocks], axis=1).astype(bf16)


def _prep_w_uk(w_uk):
    t = w_uk.reshape(MLA_KV_LORA, MLA_HEADS, MLA_NOPE).transpose(1, 2, 0)
    return jnp.pad(t, ((0, 0), (0, LANES - MLA_NOPE), (0, 0))).astype(bf16)


def _prep_w_uv(w_uv):
    per = w_uv.reshape(MLA_KV_LORA, MLA_HEADS, MLA_V)
    out = jnp.zeros((MLA_HEADS, MLA_KV_LORA, MLA_HEADS * MLA_V), w_uv.dtype)
    for h in range(MLA_HEADS):
        out = out.at[h, :, h * MLA_V:(h + 1) * MLA_V].set(per[:, h, :])
    return out.astype(bf16)


def _prep_compress(pos_k, w1_k, w2_k, pos_v, w1_v, w2_v):
    def w1(w):
        return w.reshape(2, CMP_STRIDE * HEAD_DIM, CMP_HID).astype(bf16)

    def pe(p):
        return p.reshape(2, CMP_STRIDE * HEAD_DIM)

    zl = jnp.zeros_like(w2_k)
    return dict(pek=pe(pos_k), pev=pe(pos_v), w1k=w1(w1_k), w1v=w1(w1_v),
                w2kl=jnp.concatenate([w2_k, zl], 1).astype(bf16), w2kr=jnp.concatenate([zl, w2_k], 1).astype(bf16),
                w2vl=jnp.concatenate([w2_v, zl], 1).astype(bf16), w2vr=jnp.concatenate([zl, w2_v], 1).astype(bf16))


def _prep_w_out(w):
    na = SB_HEADS * HEAD_DIM
    nb = MLA_HEADS * MLA_V
    wa = jnp.concatenate([w[hh * HEAD_DIM:(hh + 1) * HEAD_DIM] for hh in _SB_HEAD_ORDER], axis=0)
    return wa.astype(bf16), w[na:na + nb].astype(bf16), w[na + nb:].astype(bf16)


def _rope_tables(pos):
    half = MLA_ROPE // 2
    inv = ROPE_BASE ** (-jnp.arange(half, dtype=f32) / half)
    ang = pos.astype(f32)[:, None] * inv[None, :]
    pad = ((0, 0), (0, LANES - MLA_ROPE))
    cos = jnp.cos(ang)
    sin = jnp.sin(ang)
    return jnp.pad(jnp.concatenate([cos, cos], 1), pad), jnp.pad(jnp.concatenate([sin, sin], 1), pad)


def _layer_weights(W, l):
    return dict(
        w_in=_prep_w_in(W["w_in"][l]), wq=_prep_w_uq(W["w_uq"][l]), wuk=_prep_w_uk(W["w_uk"][l]),
        wuv=_prep_w_uv(W["w_uv"][l]), gq=W["g_q_lora"][l], gkv=W["g_kv_lora"][l],
        cw=_prep_compress(W["cmp_pos_k"][l], W["cmp_w1_k"][l], W["cmp_w2_k"][l],
                          W["cmp_pos_v"][l], W["cmp_w1_v"][l], W["cmp_w2_v"][l]),
        w_out=_prep_w_out(W["w_out"][l]),
        g_attn_pre=W["g_attn_pre"][l], g_attn_post=W["g_attn_post"][l],
        g_ffn_pre=W["g_ffn_pre"][l], g_ffn_post=W["g_ffn_post"][l],
        w_gate=W["w_gate"][l].astype(bf16), w_up=W["w_up"][l].astype(bf16), w_down=W["w_down"][l].astype(bf16))


def _cache_rows(pf, ckvn, krn):
    c = _COL
    hd = HEAD_DIM
    return dict(sb_k=pf[:, c["sbk"]:c["sbk"] + LANES], sb_v=pf[:, c["sbv"]:c["sbv"] + LANES],
                ckv=ckvn, kr=krn[:, :MLA_ROPE],
                ck=pf[:, c["ckcv"]:c["ckcv"] + hd], cv=pf[:, c["ckcv"] + hd:c["ckcv"] + 2 * hd],
                sk=pf[:, c["sksv"]:c["sksv"] + hd], sv=pf[:, c["sksv"] + hd:c["sksv"] + 2 * hd],
                wk=pf[:, c["wkwv"]:c["wkwv"] + hd], wv=pf[:, c["wkwv"] + hd:c["wkwv"] + 2 * hd])


def _prompt_layer(x, lw, b, t, tabs):
    cos_t, sin_t, rel_bias, tsw, tcmp = tabs
    pf, pb = _proj(x, lw["g_attn_pre"], lw["w_in"])
    qcat, kcat, ckvn, krn = _mla_prep(pf, lw["gq"], lw["gkv"], lw["wq"], lw["wuk"], cos_t, sin_t,
                                      t // min(256, b * t))
    rows = _cache_rows(pf, ckvn, krn)
    nch = t // CMP_STRIDE
    kvc, vkc = _compress(rows["ck"].reshape(b, nch, CMP_STRIDE * HEAD_DIM),
                         rows["cv"].reshape(b, nch, CMP_STRIDE * HEAD_DIM), lw["cw"])
    oa = _sb_prompt(pb, b, t)
    ob = _mla_prompt(qcat, kcat, lw["wuv"], b, t)
    oc = _nsa_prompt(pb, pf, kvc, vkc, rel_bias, tsw, tcmp, b, t)
    wa, wb, wc = lw["w_out"]
    x = _outproj(oa, ob, oc, wa, wb, wc, lw["g_attn_post"], x)
    x = _ffn(x, lw["g_ffn_pre"], lw["w_gate"], lw["w_up"], lw["w_down"], lw["g_ffn_post"])
    return x, rows


def _page_copy(cache_ref, buf_ref, sem_ref, page, slot, p):
    rows, cols = cache_ref.shape[1:]
    if buf_ref.shape[2] == cols:
        dst = buf_ref.at[slot, pl.ds(p * rows, rows)]
    else:
        dst = buf_ref.at[slot, :, pl.ds(p * cols, cols)]
    return pltpu.make_async_copy(cache_ref.at[page], dst, sem_ref.at[slot])


def _gather_pages(pt_ref, caches, bufs, sems, base, npg):
    s = pl.program_id(0)
    slot = s % 2

    def start(seq, sl):
        def body(p, c):
            page = base + pt_ref[seq * npg + p]
            for cache, buf, sem in zip(caches, bufs, sems):
                _page_copy(cache, buf, sem, page, sl, p).start()
            return c
        lax.fori_loop(0, npg, body, 0)

    @pl.when(s == 0)
    def _():
        start(0, 0)

    @pl.when(s + 1 < pl.num_programs(0))
    def _():
        start(s + 1, 1 - slot)

    def wait_body(p, c):
        for cache, buf, sem in zip(caches, bufs, sems):
            _page_copy(cache, buf, sem, 0, slot, p).wait()
        return c
    lax.fori_loop(0, npg, wait_body, 0)
    return slot


def _decode_bias_kernel(rb_ref, tsel_ref, twin_ref, tcmp_ref, tnew_ref, *, past):
    def rowdist(width, fn):
        return jnp.maximum(fn(_iota((1, width), 1)), 0)

    dsel = rowdist(tsel_ref.shape[1], lambda c: tsel_ref.shape[1] - c)
    dwin = rowdist(twin_ref.shape[1], lambda c: twin_ref.shape[1] - c)
    dcmp = rowdist(tcmp_ref.shape[1], lambda n: past - (CMP_STRIDE * n + CMP_LEN - 1))
    lane = _iota((1, LANES), 1)
    for h in range(NSA_HEADS):
        tsel_ref[h:h + 1, :] = _bias_lookup(dsel, rb_ref, h)
        twin_ref[h:h + 1, :] = _bias_lookup(dwin, rb_ref, h)
        tcmp_ref[h:h + 1, :] = _bias_lookup(dcmp, rb_ref, h)
        tnew_ref[h:h + 1, :] = jnp.where(lane < HEAD_DIM, rb_ref[h], rb_ref[(N_BUCKETS - 1) * NSA_HEADS + h])


def _decode_bias_tables(rel_bias, past, ck_sel, pw, nch):
    shp = lambda w: jax.ShapeDtypeStruct((NSA_HEADS, w), f32)
    return pl.pallas_call(
        functools.partial(_decode_bias_kernel, past=past),
        out_shape=(shp(ck_sel), shp(pw), shp(nch), shp(LANES)),
        in_specs=[pl.BlockSpec(memory_space=pltpu.SMEM)],
    )(rel_bias.reshape(-1))


def _sb_decode_kernel(pt_ref, q_ref, kc_ref, vc_ref, o_ref, kbuf, vbuf, sems, *, base, npg, ck):
    slot = _gather_pages(pt_ref, (kc_ref, vc_ref), (kbuf, vbuf), (sems.at[0], sems.at[1]), base, npg)
    past = kbuf.shape[2]
    qg = q_ref[0]
    lane = _iota(qg.shape, 1)
    zero = jnp.zeros_like(qg)
    qm = jnp.concatenate([jnp.where(lane < HEAD_DIM, qg, zero), jnp.where(lane < HEAD_DIM, zero, qg),
                          jnp.zeros((4, LANES), f32)], axis=0).astype(bf16)
    tri = jnp.where(_iota((ck, ck), 0) > _iota((ck, ck), 1), 1.0, 0.0).astype(bf16)
    scale = HEAD_DIM ** -0.5
    nchunk = past // ck

    zs = [_dot(qm, kbuf[slot, :, c * ck:(c + 1) * ck].astype(bf16)) * scale for c in range(nchunk)]
    lrs = [-_softplus(z) for z in zs]
    lr_all = jnp.concatenate(lrs, axis=0)
    hi = lr_all.astype(bf16)
    lo = (lr_all - hi.astype(f32)).astype(bf16)
    inner = _dot(hi, tri) + _dot(lo, tri)
    acc = jnp.zeros((8, LANES), f32)
    run = jnp.zeros((8, 1), f32)
    for c in reversed(range(nchunk)):
        w = jnp.exp(zs[c] + lrs[c] + inner[8 * c:8 * c + 8] + run)
        acc = acc + _dot_nt(w.astype(bf16), vbuf[slot, :, c * ck:(c + 1) * ck].astype(bf16))
        run = run + jnp.sum(lrs[c], axis=-1, keepdims=True)
    o_ref[0] = jnp.where(lane < HEAD_DIM, acc[0:2], acc[2:4])


def _sb_decode(q, cache_k, cache_v, pt, layer):
    s = q.shape[0]
    depth, n_pool, page, _, _ = cache_k.shape
    npg = pt.shape[1]
    past = npg * page
    ck = min(512, past)
    ck2 = lambda c: jnp.transpose(c, (0, 1, 3, 4, 2)).reshape(depth * n_pool, LANES, page)
    grid_spec = pltpu.PrefetchScalarGridSpec(
        num_scalar_prefetch=1, grid=(s,),
        in_specs=[pl.BlockSpec((1, 2, LANES), lambda i, pt_: (i, 0, 0)),
                  pl.BlockSpec(memory_space=pl.ANY), pl.BlockSpec(memory_space=pl.ANY)],
        out_specs=pl.BlockSpec((1, 2, LANES), lambda i, pt_: (i, 0, 0)),
        scratch_shapes=[pltpu.VMEM((2, LANES, past), f32), pltpu.VMEM((2, LANES, past), f32),
                        pltpu.SemaphoreType.DMA((2, 2))])
    out = pl.pallas_call(
        functools.partial(_sb_decode_kernel, base=layer * n_pool, npg=npg, ck=ck),
        grid_spec=grid_spec,
        out_shape=jax.ShapeDtypeStruct((s, 2, LANES), f32),
        compiler_params=_cparams(("arbitrary",)),
    )(pt.reshape(-1), q.reshape(s, 2, LANES), ck2(cache_k), ck2(cache_v))
    return out.reshape(s, 2 * LANES).astype(bf16)


def _mla_decode_kernel(pt_ref, q_ref, kn_ref, wuv_ref, cc_ref, rc_ref, o_ref, cbuf, rbuf, sems, *, base, npg, ck):
    slot = _gather_pages(pt_ref, (cc_ref, rc_ref), (cbuf, rbuf), (sems.at[0], sems.at[1]), base, npg)
    past = cbuf.shape[1]
    nh = MLA_HEADS
    q = jnp.concatenate([q_ref[0], jnp.zeros((8 - nh, 2 * LANES), f32)], axis=0)
    qlat = q[:, :LANES].astype(bf16)
    qrope = q[:, LANES:LANES + MLA_ROPE].astype(bf16)
    scale = (MLA_NOPE + MLA_ROPE) ** -0.5

    kn = kn_ref[0]
    s_new = jnp.sum(q * kn, axis=-1, keepdims=True) * scale
    nchunk = past // ck
    lat = [cbuf[slot, c * ck:(c + 1) * ck, :].astype(bf16) for c in range(nchunk)]
    ss = [(_dot_nt(qlat, lat[c]) + _dot(qrope, rbuf[slot, :, c * ck:(c + 1) * ck].astype(bf16))) * scale
          for c in range(nchunk)]
    m = s_new
    for s in ss:
        m = jnp.maximum(m, jnp.max(s, axis=-1, keepdims=True))
    p_new = jnp.exp(s_new - m)
    l = p_new
    acc = p_new * kn[:, :LANES]
    for c in range(nchunk):
        p = jnp.exp(ss[c] - m)
        l = l + jnp.sum(p, axis=-1, keepdims=True)
        acc = acc + _dot(p.astype(bf16), lat[c])
    olat = (acc / l).astype(bf16)
    out = _dot(olat[0:1], wuv_ref[0])
    for h in range(1, nh):
        out += _dot(olat[h:h + 1], wuv_ref[h])
    o_ref[0] = out


def _mla_decode(qcat, kcat, wuv, cache_c, cache_r, pt, layer):
    s = qcat.shape[0]
    depth, n_pool, page, _ = cache_c.shape
    npg = pt.shape[1]
    past = npg * page
    ck = min(1024, past)
    grid_spec = pltpu.PrefetchScalarGridSpec(
        num_scalar_prefetch=1, grid=(s,),
        in_specs=[pl.BlockSpec((1, MLA_HEADS, 2 * LANES), lambda i, pt_: (i, 0, 0)),
                  pl.BlockSpec((1, 1, 2 * LANES), lambda i, pt_: (i, 0, 0)),
                  pl.BlockSpec(wuv.shape, lambda i, pt_: (0, 0, 0)),
                  pl.BlockSpec(memory_space=pl.ANY), pl.BlockSpec(memory_space=pl.ANY)],
        out_specs=pl.BlockSpec((1, 1, MLA_HEADS * MLA_V), lambda i, pt_: (i, 0, 0)),
        scratch_shapes=[pltpu.VMEM((2, past, MLA_KV_LORA), f32), pltpu.VMEM((2, MLA_ROPE, past), f32),
                        pltpu.SemaphoreType.DMA((2, 2))])
    out = pl.pallas_call(
        functools.partial(_mla_decode_kernel, base=layer * n_pool, npg=npg, ck=ck),
        grid_spec=grid_spec,
        out_shape=jax.ShapeDtypeStruct((s, 1, MLA_HEADS * MLA_V), f32),
        compiler_params=_cparams(("arbitrary",)),
    )(pt.reshape(-1), qcat.astype(f32).reshape(s, MLA_HEADS, 2 * LANES), kcat.astype(f32).reshape(s, 1, 2 * LANES),
      wuv, cache_c.reshape(depth * n_pool, page, MLA_KV_LORA),
      jnp.swapaxes(cache_r, 2, 3).reshape(depth * n_pool, MLA_ROPE, page))
    return out.reshape(s, MLA_HEADS * MLA_V).astype(bf16)


def _softmax_parts(s, mask):
    sm = s if mask is None else jnp.where(mask, s, NEG_INF)
    m = jnp.max(sm, axis=-1, keepdims=True)
    p = jnp.exp(s - m)
    if mask is not None:
        p = jnp.where(mask, p, 0.0)
    return m, p


def _nsa_decode_kernel(pt_ref, q_ref, misc_ref, snew_ref, wnew_ref, wink_ref, winv_ref,
                       pek_ref, pev_ref, w1k_ref, w1v_ref, w2kl_ref, w2kr_ref, w2vl_ref, w2vr_ref,
                       amat_ref, tsel_ref, twin_ref, tcmp_ref, tnew_ref,
                       ckc_ref, cvc_ref, skc_ref, svc_ref, o_ref,
                       ckbuf, cvbuf, skbuf, svbuf, sems, *, base, npg, ck, k_top):
    slot = _gather_pages(pt_ref, (ckc_ref, cvc_ref, skc_ref, svc_ref), (ckbuf, cvbuf, skbuf, svbuf),
                         tuple(sems.at[n] for n in range(4)), base, npg)
    past = skbuf.shape[2]
    nh = NSA_HEADS
    hd = HEAD_DIM
    scale = hd ** -0.5
    q = q_ref[0]
    qb = q.astype(bf16)
    qpad = jnp.concatenate([q, jnp.zeros_like(q)], axis=1).astype(bf16)
    rowi = _iota((nh, LANES), 0)
    lanei = _iota((nh, LANES), 1)
    gates = 1.0 / (1.0 + jnp.exp(-misc_ref[0]))

    def gate(branch):
        pick = lanei == GATE_LANE + branch * nh + rowi
        return jnp.sum(jnp.where(pick, jnp.broadcast_to(gates, (nh, LANES)), 0.0), axis=-1, keepdims=True)

    nch = ckbuf.shape[1] // CMP_STRIDE
    kv, vk = _compress_finish(_strided_halves(ckbuf, slot, nch, pek_ref, w1k_ref),
                              _strided_halves(cvbuf, slot, nch, pev_ref, w1v_ref),
                              w2kl_ref, w2kr_ref, w2vl_ref, w2vr_ref)
    ncol = _iota((nh, nch), 1)
    okc = (CMP_STRIDE * ncol + (CMP_LEN - 1)) <= past
    sc = _dot_nt(qpad, kv.astype(bf16)) * scale + tcmp_ref[...]
    _, p = _softmax_parts(sc, okc)
    l = jnp.sum(p, axis=-1, keepdims=True)
    pc = p * jnp.where(l > 0.0, 1.0 / l, 0.0)
    oc = _dot(pc.astype(bf16), vk.astype(bf16))[:, :hd]
    imp = jnp.broadcast_to(jnp.sum(pc, axis=0, keepdims=True), (nh, nch))

    amat = amat_ref[...]
    nsc = amat.shape[1]
    hi, mid, lo = _split3(imp)
    p_slc = (_dot(hi, amat) + _dot(mid, amat) + _dot(lo, amat))[0:1]
    jl = _iota((1, nsc), 1)
    cur = past // SEL_BLOCK
    forced = (jl == 0) | (jl == cur) | (jl == cur - 1)
    score = jnp.where(jl <= cur, jnp.where(forced, FORCE_SCORE, p_slc), -1.0)
    ii = _iota((nsc, nsc), 0)
    jj = _iota((nsc, nsc), 1)
    srow = jnp.broadcast_to(score, (nsc, nsc))
    scol = jnp.sum(jnp.where(ii == jj, srow, 0.0), axis=-1, keepdims=True)
    beats = jnp.where(scol > srow, 1.0, jnp.where((scol == srow) & (ii < jj), 1.0, 0.0))
    rank = jnp.sum(beats, axis=0, keepdims=True)
    sel = jnp.where((rank < k_top) & (score >= 0.0), 1.0, 0.0)
    sel8 = jnp.broadcast_to(sel, (nh, nsc)).astype(bf16)

    nchunk = past // ck
    bpc = ck // SEL_BLOCK
    jb = _iota((nsc, ck), 0)
    cb = _iota((nsc, ck), 1) // SEL_BLOCK
    far = tnew_ref[:, hd:hd + 1]

    masks, ss = [], []
    m = jnp.full((nh, 1), NEG_INF, f32)
    for c in range(nchunk):
        emat = jnp.where(jb == c * bpc + cb, 1.0, 0.0).astype(bf16)
        mask = _dot(sel8, emat) > 0.5
        bias = tsel_ref[...] if c == nchunk - 1 else far
        s = _dot(qb, skbuf[slot, :, c * ck:(c + 1) * ck].astype(bf16)) * scale + bias
        m = jnp.maximum(m, jnp.max(jnp.where(mask, s, NEG_INF), axis=-1, keepdims=True))
        masks.append(mask)
        ss.append(s)
    l = jnp.zeros((nh, 1), f32)
    acc = jnp.zeros((nh, hd), f32)
    for c in range(nchunk):
        p = jnp.where(masks[c], jnp.exp(ss[c] - m), 0.0)
        l = l + jnp.sum(p, axis=-1, keepdims=True)
        acc = acc + _dot_nt(p.astype(bf16), svbuf[slot, :, c * ck:(c + 1) * ck].astype(bf16))

    def add_own(m, l, acc, k_new, v_new, valid):
        s_new = jnp.sum(q * k_new, axis=-1, keepdims=True) * scale + tnew_ref[:, 0:1]
        s_new = jnp.where(valid, s_new, NEG_INF)
        m_f = jnp.maximum(m, s_new)
        p_new = jnp.where(valid, jnp.exp(s_new - m_f), 0.0)
        a = jnp.exp(m - m_f)
        return (a * acc + p_new * v_new) / (a * l + p_new)

    round_b = lambda x: x.astype(bf16).astype(f32)
    snew = round_b(snew_ref[0])
    own_sel = jnp.sum(jnp.where(jl == cur, sel, 0.0), axis=-1, keepdims=True) > 0.5
    osel = add_own(m, l, acc, snew[:, :hd], snew[:, hd:], own_sel)

    kw = wink_ref[...].astype(bf16)
    vw = winv_ref[...].astype(bf16)
    pw = kw.shape[1]
    s = _dot(qb, kw) * scale + twin_ref[...]
    inwin = _iota((nh, pw), 1) > pw - WINDOW
    m, p = _softmax_parts(s, inwin)
    wnew = round_b(wnew_ref[0])
    ow = add_own(m, jnp.sum(p, axis=-1, keepdims=True), _dot_nt(p.astype(bf16), vw), wnew[:, :hd], wnew[:, hd:], True)

    o_ref[0] = gate(0) * oc + gate(1) * osel + gate(2) * ow


def _sel_matrix_decode(nch, n_cmp, ncols):
    a = np.zeros((nch, ncols), np.float32)
    for n in range(n_cmp):
        a[n, n // CH_PER_SEL] += 0.5
        a[n, (n + 1) // CH_PER_SEL] += 0.5
    return a


def _nsa_decode(pf, cw, dtabs, caches, win_k, win_v, pt, layer):
    cache_ck, cache_cv, cache_sk, cache_sv = caches
    s = pf.shape[0]
    depth, n_pool, page, hd = cache_ck.shape
    npg = pt.shape[1]
    past = npg * page
    nch = (past + 1) // CMP_STRIDE
    n_sel = -(-(past + 1) // SEL_BLOCK)
    nsc = -(-n_sel // LANES) * LANES
    ck = min(1024, past)
    pw = win_k.shape[2]
    amat = jnp.asarray(_sel_matrix_decode(nch, nch - 1, nsc), bf16)
    tsel, twin, tcmp, tnew = dtabs
    c = _COL
    q = pf[:, c["nq"]:c["nq"] + NSA_HEADS * hd].reshape(s, NSA_HEADS, hd)
    row3 = lambda name: pf[:, c[name]:c[name] + LANES].reshape(s, 1, LANES)
    seq3 = lambda shape: pl.BlockSpec((1,) + shape, lambda i, pt_: (i, 0, 0))
    fixed = lambda a: pl.BlockSpec(a.shape, lambda i, pt_, n=a.ndim: (0,) * n)
    chunks_per_page = page // CMP_STRIDE
    grid_spec = pltpu.PrefetchScalarGridSpec(
        num_scalar_prefetch=1, grid=(s,),
        in_specs=[seq3((NSA_HEADS, hd)), seq3((1, LANES)), seq3((1, LANES)), seq3((1, LANES)),
                  pl.BlockSpec((None, None, hd, pw), lambda i, pt_: (layer, i, 0, 0)),
                  pl.BlockSpec((None, None, hd, pw), lambda i, pt_: (layer, i, 0, 0))]
                 + [fixed(a) for a in _compress_args(cw)] + [fixed(a) for a in (amat, tsel, twin, tcmp, tnew)]
                 + [pl.BlockSpec(memory_space=pl.ANY)] * 4,
        out_specs=seq3((NSA_HEADS, hd)),
        scratch_shapes=[pltpu.VMEM((2, past, hd), f32), pltpu.VMEM((2, past, hd), f32),
                        pltpu.VMEM((2, hd, past), f32), pltpu.VMEM((2, hd, past), f32),
                        pltpu.SemaphoreType.DMA((4, 2))])
    chunked = lambda a: a.reshape(depth * n_pool, page, hd)
    paged = lambda a: jnp.swapaxes(a, 2, 3).reshape(depth * n_pool, hd, page)
    out = pl.pallas_call(
        functools.partial(_nsa_decode_kernel, base=layer * n_pool, npg=npg, ck=ck, k_top=min(SEL_TOPK, n_sel)),
        grid_spec=grid_spec,
        out_shape=jax.ShapeDtypeStruct((s, NSA_HEADS, hd), f32),
        compiler_params=_cparams(("arbitrary",)),
    )(pt.reshape(-1), q, row3("misc"), row3("sksv"), row3("wkwv"), jnp.swapaxes(win_k, 2, 3), jnp.swapaxes(win_v, 2, 3),
      *_compress_args(cw), amat, tsel, twin, tcmp, tnew,
      chunked(cache_ck), chunked(cache_cv), paged(cache_sk), paged(cache_sv))
    return out.reshape(s, NSA_HEADS * hd).astype(bf16)


def _sample_layer(x, lw, tabs, caches, pt, layer):
    cos_t, sin_t, dtabs = tabs
    (c_sbk, c_sbv, c_ckv, c_kr, c_ck, c_cv, c_sk, c_sv, win_k, win_v) = caches
    pf, _ = _proj(x, lw["g_attn_pre"], lw["w_in"])
    qcat, kcat, ckvn, krn = _mla_prep(pf, lw["gq"], lw["gkv"], lw["wq"], lw["wuk"], cos_t, sin_t, 1)
    rows = _cache_rows(pf, ckvn, krn)
    oa = _sb_decode(pf[:, _COL["sbq"]:_COL["sbq"] + 2 * LANES], c_sbk, c_sbv, pt, layer)
    ob = _mla_decode(qcat, kcat, lw["wuv"], c_ckv, c_kr, pt, layer)
    oc = _nsa_decode(pf, lw["cw"], dtabs, (c_ck, c_cv, c_sk, c_sv), win_k, win_v, pt, layer)
    wa, wb, wc = lw["w_out"]
    x = _outproj(oa, ob, oc, wa, wb, wc, lw["g_attn_post"], x)
    x = _ffn(x, lw["g_ffn_pre"], lw["w_gate"], lw["w_up"], lw["w_down"], lw["g_ffn_post"])
    keep = min(WINDOW, win_k.shape[2] + 1)
    rows["wk"] = jnp.concatenate([win_k[layer], rows["wk"][:, None, :]], axis=1)[:, -keep:]
    rows["wv"] = jnp.concatenate([win_v[layer], rows["wv"][:, None, :]], axis=1)[:, -keep:]
    return x, rows


def kernel(x_prompt, x_sample, cache_sb_k, cache_sb_v, cache_mla_ckv, cache_mla_kr, cache_nsa_cmp_k, cache_nsa_cmp_v,
           cache_nsa_sel_k, cache_nsa_sel_v, state_nsa_win_k, state_nsa_win_v, page_table,
           w_in, g_q_lora, w_uq, g_kv_lora, w_uk, w_uv, cmp_pos_k, cmp_w1_k, cmp_w2_k, cmp_pos_v, cmp_w1_v, cmp_w2_v,
           rel_bias, w_out, g_attn_pre, g_attn_post, g_ffn_pre, g_ffn_post, w_gate, w_up, w_down):
    W = dict(w_in=w_in, g_q_lora=g_q_lora, w_uq=w_uq, g_kv_lora=g_kv_lora, w_uk=w_uk, w_uv=w_uv,
             cmp_pos_k=cmp_pos_k, cmp_w1_k=cmp_w1_k, cmp_w2_k=cmp_w2_k,
             cmp_pos_v=cmp_pos_v, cmp_w1_v=cmp_w1_v, cmp_w2_v=cmp_w2_v,
             w_out=w_out, g_attn_pre=g_attn_pre, g_attn_post=g_attn_post,
             g_ffn_pre=g_ffn_pre, g_ffn_post=g_ffn_post, w_gate=w_gate, w_up=w_up, w_down=w_down)
    depth = w_in.shape[0]
    b, t, d = x_prompt.shape
    s = x_sample.shape[0]
    page = cache_sb_k.shape[2]
    past = page_table.shape[1] * page
    lws = [_layer_weights(W, l) for l in range(depth)]

    cos_p, sin_p = _rope_tables(jnp.arange(t, dtype=jnp.int32))
    tsw, tcmp = _bias_tables(rel_bias, t // CMP_STRIDE)
    y = x_prompt.reshape(b * t, d)
    rows_p = []
    for l in range(depth):
        y, rows = _prompt_layer(y, lws[l], b, t, (cos_p, sin_p, rel_bias, tsw, tcmp))
        rows_p.append(rows)
    y_prompt = y.reshape(b, t, d)

    cos_s, sin_s = _rope_tables(jnp.full((s,), past, jnp.int32))
    dtabs = _decode_bias_tables(rel_bias, past, min(1024, past), state_nsa_win_k.shape[2], (past + 1) // CMP_STRIDE)
    caches = (cache_sb_k, cache_sb_v, cache_mla_ckv, cache_mla_kr, cache_nsa_cmp_k, cache_nsa_cmp_v,
              cache_nsa_sel_k, cache_nsa_sel_v, state_nsa_win_k, state_nsa_win_v)
    y = x_sample.reshape(s, d)
    rows_s = []
    for l in range(depth):
        y, rows = _sample_layer(y, lws[l], (cos_s, sin_s, dtabs), caches, page_table, l)
        rows_s.append(rows)
    y_sample = y.reshape(s, 1, d)

    keep_p = min(WINDOW, t)

    def stack(rows_all, name, shape, tail=None):
        parts = [r[name].reshape(shape) for r in rows_all]
        if tail is not None:
            parts = [p[:, -tail:] for p in parts]
        return jnp.stack(parts, axis=0)

    def outputs(rows_all, n, tt, win_tail):
        kvh = (n, tt, SB_KV_HEADS, HEAD_DIM)
        return (stack(rows_all, "sb_k", kvh), stack(rows_all, "sb_v", kvh),
                stack(rows_all, "ckv", (n, tt, MLA_KV_LORA)), stack(rows_all, "kr", (n, tt, MLA_ROPE)),
                stack(rows_all, "ck", (n, tt, HEAD_DIM)), stack(rows_all, "cv", (n, tt, HEAD_DIM)),
                stack(rows_all, "sk", (n, tt, HEAD_DIM)), stack(rows_all, "sv", (n, tt, HEAD_DIM)),
                stack(rows_all, "wk", (n, -1, HEAD_DIM), win_tail), stack(rows_all, "wv", (n, -1, HEAD_DIM), win_tail))

    return (y_prompt, y_sample) + outputs(rows_p, b, t, keep_p) + outputs(rows_s, s, 1, None)
```
